```python
import math
import jax
import jax.numpy as jnp
from jax import lax
import numpy as np

D_MODEL = 1024
BATCH = 4
SEQ = 8192
DEPTH = 4

CHUNK = 64
N_MIXERS = 3
HEAD_DIM = 64
N_HEADS = D_MODEL // HEAD_DIM
D_FF = -(-8 * D_MODEL // (3 * 256)) * 256
RMS_EPS = 1e-6
ROPE_THETA = 10000.0
RW_DECAY_LORA = max(32, int(round(1.8 * D_MODEL ** 0.5 / 32)) * 32)
RW_AAA_LORA = max(32, int(round(1.8 * D_MODEL ** 0.5 / 32)) * 32)
RW_MV_LORA = max(32, int(round(1.3 * D_MODEL ** 0.5 / 32)) * 32)
RW_GATE_LORA = max(32, int(round(0.6 * D_MODEL ** 0.8 / 32)) * 32)
RW_GN_EPS = HEAD_DIM * 1e-5
SB_QBLOCK = 128
DS_TOPK_MAX = 256
DS_IDX_HEADS = 8
DS_IDX_DIM = 64
DS_QBLOCK = CHUNK
DS_IN_WIDTH = 3 * D_MODEL + DS_IDX_HEADS * DS_IDX_DIM + DS_IDX_DIM + DS_IDX_HEADS
N_RWKV = (DEPTH + 2) // 3
N_SB = (DEPTH + 1) // 3
N_DSA = DEPTH // 3

kernel_name = 'hybrid_rwkv7_stickbreak_dsa'


def rms_norm(x, g):
    xf = x.astype(jnp.float32)
    y = xf * lax.rsqrt(jnp.mean(xf * xf, axis=-1, keepdims=True) + RMS_EPS)
    return (y * g.astype(jnp.float32)).astype(x.dtype)


def rope(x, pos):
    half = x.shape[-1] // 2
    inv = 1.0 / (ROPE_THETA ** (jnp.arange(half, dtype=jnp.float32) / half))
    ang = pos.astype(jnp.float32)[:, None] * inv[None, :]
    cos = jnp.cos(ang)[None, :, None, :]
    sin = jnp.sin(ang)[None, :, None, :]
    xf = x.astype(jnp.float32)
    x1, x2 = xf[..., :half], xf[..., half:]
    return jnp.concatenate([x1 * cos - x2 * sin, x1 * sin + x2 * cos], axis=-1).astype(x.dtype)


def swiglu(h, w_gu, w_down):
    g, u = jnp.split(h @ w_gu, 2, axis=-1)
    return (jax.nn.silu(g) * u) @ w_down


def rwkv7_time_mix(h, v_first, mix, w_rkv, w0, w1, w2, a0, a1, a2, g1, g2, k_k, k_a, r_k, ln_w, ln_b, w_out, v_res):
    B, S, D = h.shape
    f32 = jnp.float32
    shifted = jnp.pad(h, ((0, 0), (1, 0), (0, 0)))[:, :-1]
    xm = h[:, :, None, :] + (shifted - h)[:, :, None, :] * mix
    rkv = jnp.einsum('bsjd,jde->bsje', xm[:, :, :3], w_rkv)
    r, k, v = rkv[:, :, 0], rkv[:, :, 1], rkv[:, :, 2]
    xv, xw, xa, xg = xm[:, :, 2], xm[:, :, 3], xm[:, :, 4], xm[:, :, 5]
    logw = -jax.nn.softplus(-(w0 + jnp.tanh(xw @ w1) @ w2).astype(f32)) - 0.5
    decay = jnp.exp(-jnp.exp(logw))
    a = jax.nn.sigmoid((a0 + (xa @ a1) @ a2).astype(f32))
    g = jax.nn.sigmoid(xg @ g1) @ g2
    v_layer = v
    if v_res is not None:
        v0, v1, v2 = v_res
        v = v + (v_first - v) * jax.nn.sigmoid(v0 + (xv @ v1) @ v2)

    def heads(t):
        return t.astype(f32).reshape(B, S, N_HEADS, HEAD_DIM)

    kk = heads(k * k_k)
    kk = kk * lax.rsqrt(jnp.maximum(jnp.sum(kk * kk, axis=-1, keepdims=True), 1e-24))
    k = k.astype(f32) * (1.0 + (a - 1.0) * k_a)
    rh, kh, vh, wh, ah = heads(r), heads(k), heads(v), heads(decay), heads(a)

    def step(state, inp):
        r_t, w_t, k_t, v_t, kk_t, a_t = inp
        sa = jnp.einsum('bhvk,bhk->bhv', state, kk_t)
        state = (state * w_t[:, :, None, :]
                 - sa[..., None] * (kk_t * a_t)[:, :, None, :]
                 + v_t[..., None] * k_t[:, :, None, :])
        return state, jnp.einsum('bhvk,bhk->bhv', state, r_t)

    def tmaj(t):
        return jnp.moveaxis(t, 1, 0)

    s0 = jnp.zeros((B, N_HEADS, HEAD_DIM, HEAD_DIM), f32)
    _, y = lax.scan(step, s0, (tmaj(rh), tmaj(wh), tmaj(kh), tmaj(vh), tmaj(kk), tmaj(ah)))
    y = jnp.moveaxis(y, 0, 1)
    mu = jnp.mean(y, axis=-1, keepdims=True)
    var = jnp.mean(jnp.square(y - mu), axis=-1, keepdims=True)
    y = ((y - mu) * lax.rsqrt(var + RW_GN_EPS)).reshape(B, S, D) * ln_w + ln_b
    y = y + (jnp.sum(rh * kh * r_k, axis=-1, keepdims=True) * vh).reshape(B, S, D)
    out = (y * g).astype(h.dtype) @ w_out
    return out, v_layer


def stick_breaking_attention(h, w_qkv, w_out):
    B, S, D = h.shape
    f32 = jnp.float32
    qkv = (h @ w_qkv).reshape(B, S, 3, N_HEADS, HEAD_DIM)
    q = qkv[:, :, 0].astype(f32) * HEAD_DIM ** -0.5
    k = qkv[:, :, 1].astype(f32)
    v = qkv[:, :, 2]
    nb = S // SB_QBLOCK
    q_blocks = jnp.moveaxis(q.reshape(B, nb, SB_QBLOCK, N_HEADS, HEAD_DIM), 1, 0)
    kpos = jnp.arange(S)

    def block(args):
        q_b, i = args
        qpos = i * SB_QBLOCK + jnp.arange(SB_QBLOCK)
        earlier = kpos[None, :] < qpos[:, None]
        z = jnp.einsum('bqhd,bshd->bhqs', q_b, k)
        log_1m = jnp.where(earlier, jax.nn.log_sigmoid(-z), 0.0)
        between = lax.cumsum(log_1m, axis=3, reverse=True) - log_1m
        att = jnp.where(earlier, jnp.exp(jax.nn.log_sigmoid(z) + between), 0.0)
        return jnp.einsum('bhqs,bshd->bqhd', att.astype(v.dtype), v)

    o = lax.map(block, (q_blocks, jnp.arange(nb)))
    o = jnp.moveaxis(o, 0, 1).reshape(B, S, D)
    return o @ w_out


def dsa_attention(h, w_in, q_norm, k_norm, w_out):
    B, S, D = h.shape
    f32 = jnp.float32
    pos = jnp.arange(S)
    c3 = 3 * D
    c4 = c3 + DS_IDX_HEADS * DS_IDX_DIM
    c5 = c4 + DS_IDX_DIM
    q, k, v, qi, ki, wi = jnp.split(h @ w_in, [D, 2 * D, c3, c4, c5], axis=-1)
    q = rope(rms_norm(q.reshape(B, S, N_HEADS, HEAD_DIM), q_norm), pos)
    k = rope(rms_norm(k.reshape(B, S, N_HEADS, HEAD_DIM), k_norm), pos)
    v = v.reshape(B, S, N_HEADS, HEAD_DIM)
    qi = rope(qi.reshape(B, S, DS_IDX_HEADS, DS_IDX_DIM), pos)
    ki = rope(ki.reshape(B, S, 1, DS_IDX_DIM), pos)[:, :, 0]
    wi = wi * (DS_IDX_HEADS ** -0.5 * DS_IDX_DIM ** -0.5)
    n_sel = min(DS_TOPK_MAX, S // 4)
    nb = S // DS_QBLOCK

    def blocks(t):
        return jnp.moveaxis(t.reshape(B, nb, DS_QBLOCK, *t.shape[2:]), 1, 0)

    kpos = jnp.arange(S)

    def block(args):
        q_b, qi_b, wi_b, i = args
        qpos = i * DS_QBLOCK + jnp.arange(DS_QBLOCK)
        visible = kpos[None, :] < ((qpos // CHUNK + 1) * CHUNK)[:, None]
        idx = jnp.einsum('bqhd,bsd->bqhs', qi_b, ki)
        score = jnp.einsum('bqh,bqhs->bqs', wi_b, jax.nn.relu(idx)).astype(f32)
        score = jnp.where(visible[None], score, -jnp.inf)
        top_val, top_idx = lax.top_k(score, n_sel)
        valid = top_val > -jnp.inf
        k_sel = jax.vmap(lambda kb, ib: kb[ib])(k, top_idx)
        v_sel = jax.vmap(lambda vb, ib: vb[ib])(v, top_idx)
        logits = jnp.einsum('bqhd,bqkhd->bhqk', q_b, k_sel).astype(f32) * HEAD_DIM ** -0.5
        logits = jnp.where(valid[:, None], logits, -jnp.inf)
        p = jax.nn.softmax(logits, axis=-1)
        return jnp.einsum('bhqk,bqkhd->bqhd', p.astype(v.dtype), v_sel)

    o = lax.map(block, (blocks(q), blocks(qi), blocks(wi), jnp.arange(nb)))
    o = jnp.moveaxis(o, 0, 1).reshape(B, S, D)
    return o @ w_out


def setup_inputs(seed: int = 0) -> dict:
    key = jax.random.key(seed)
    ks = iter(jax.random.split(key, 64))
    D, F, H, dh = D_MODEL, D_FF, N_HEADS, HEAD_DIM
    nA, nB, nC = N_RWKV, N_SB, N_DSA
    out_scale = D ** -0.5 / math.sqrt(2 * DEPTH)

    def nrm(shape, scale):
        return jax.random.normal(next(ks), shape, jnp.float32) * scale

    def unif(shape, lo, hi):
        return jax.random.uniform(next(ks), shape, jnp.float32, lo, hi)

    return {
        'x': nrm((BATCH, SEQ, D), 1.0),
        'norm_mix': 1.0 + nrm((DEPTH, D), 0.02),
        'norm_ffn': 1.0 + nrm((DEPTH, D), 0.02),
        'ffn_w_gu': nrm((DEPTH, D, 2 * F), D ** -0.5),
        'ffn_w_down': nrm((DEPTH, F, D), F ** -0.5 / math.sqrt(2 * DEPTH)),
        'rw_mix': unif((nA, 6, D), 0.0, 1.0),
        'rw_w_rkv': nrm((nA, 3, D, D), D ** -0.5),
        'rw_w0': unif((nA, D), -5.0, 1.0),
        'rw_w1': nrm((nA, D, RW_DECAY_LORA), D ** -0.5),
        'rw_w2': nrm((nA, RW_DECAY_LORA, D), 0.1 * RW_DECAY_LORA ** -0.5),
        'rw_a0': nrm((nA, D), 0.1),
        'rw_a1': nrm((nA, D, RW_AAA_LORA), D ** -0.5),
        'rw_a2': nrm((nA, RW_AAA_LORA, D), 0.3 * RW_AAA_LORA ** -0.5),
        'rw_g1': nrm((nA, D, RW_GATE_LORA), D ** -0.5),
        'rw_g2': nrm((nA, RW_GATE_LORA, D), RW_GATE_LORA ** -0.5),
        'rw_v0': 1.0 + nrm((nA - 1, D), 0.1),
        'rw_v1': nrm((nA - 1, D, RW_MV_LORA), D ** -0.5),
        'rw_v2': nrm((nA - 1, RW_MV_LORA, D), 0.3 * RW_MV_LORA ** -0.5),
        'rw_k_k': 0.85 + nrm((nA, D), 0.05),
        'rw_k_a': 1.0 + nrm((nA, D), 0.05),
        'rw_r_k': nrm((nA, H, dh), 0.1),
        'rw_ln_w': 1.0 + nrm((nA, D), 0.02),
        'rw_ln_b': nrm((nA, D), 0.02),
        'rw_w_out': nrm((nA, D, D), out_scale),
        'sb_w_qkv': nrm((nB, D, 3 * D), D ** -0.5),
        'sb_w_out': nrm((nB, D, D), out_scale),
        'ds_w_in': nrm((nC, D, DS_IN_WIDTH), D ** -0.5),
        'ds_q_norm': 1.0 + nrm((nC, dh), 0.02),
        'ds_k_norm': 1.0 + nrm((nC, dh), 0.02),
        'ds_w_out': nrm((nC, D, D), out_scale),
    }


def reference(x, norm_mix, norm_ffn, ffn_w_gu, ffn_w_down,
              rw_mix, rw_w_rkv, rw_w0, rw_w1, rw_w2, rw_a0, rw_a1, rw_a2, rw_g1, rw_g2,
              rw_v0, rw_v1, rw_v2, rw_k_k, rw_k_a, rw_r_k, rw_ln_w, rw_ln_b, rw_w_out,
              sb_w_qkv, sb_w_out,
              ds_w_in, ds_q_norm, ds_k_norm, ds_w_out):
    v_first = None
    for i in range(DEPTH):
        kind, j = i % N_MIXERS, i // N_MIXERS
        h = rms_norm(x, norm_mix[i])
        if kind == 0:
            v_res = None if j == 0 else (rw_v0[j - 1], rw_v1[j - 1], rw_v2[j - 1])
            y, v_layer = rwkv7_time_mix(h, v_first, rw_mix[j], rw_w_rkv[j], rw_w0[j], rw_w1[j], rw_w2[j],
                                        rw_a0[j], rw_a1[j], rw_a2[j], rw_g1[j], rw_g2[j],
                                        rw_k_k[j], rw_k_a[j], rw_r_k[j], rw_ln_w[j], rw_ln_b[j],
                                        rw_w_out[j], v_res)
            if j == 0:
                v_first = v_layer
        elif kind == 1:
            y = stick_breaking_attention(h, sb_w_qkv[j], sb_w_out[j])
        else:
            y = dsa_attention(h, ds_w_in[j], ds_q_norm[j], ds_k_norm[j], ds_w_out[j])
        x = x + y
        x = x + swiglu(rms_norm(x, norm_ffn[i]), ffn_w_gu[i], ffn_w_down[i])
    return x
```

```python
import functools
import math

import jax
import jax.numpy as jnp
from jax import lax
from jax.experimental import pallas as pl
from jax.experimental.pallas import tpu as pltpu

F32 = jnp.float32
BF16 = jnp.bfloat16
I32 = jnp.int32

D_MODEL = 1024
HEAD_DIM = 64
N_HEADS = D_MODEL // HEAD_DIM
N_PAIRS = N_HEADS // 2
LANES = 128
RMS_EPS = 1e-6
ROPE_THETA = 10000.0
RW_GN_EPS = HEAD_DIM * 1e-5
RW_CHUNK = 64
DS_TOPK_MAX = 256
DS_IDX_HEADS = 8
DS_IDX_DIM = 64
DS_CHUNK = 64
NEG_BIG = -1e30
VMEM_LIMIT = 56 * 1024 * 1024


def _cparams(sem):
    return pltpu.CompilerParams(dimension_semantics=sem, vmem_limit_bytes=VMEM_LIMIT)


def _dot(a, b):
    return jnp.dot(a, b, preferred_element_type=F32)


def _dot_nt(a, b):
    return lax.dot_general(a, b, (((1,), (1,)), ((), ())), preferred_element_type=F32)


def _dot_tn(a, b):
    return lax.dot_general(a, b, (((0,), (0,)), ((), ())), preferred_element_type=F32)


def _split2(x):
    hi = x.astype(BF16)
    lo = (x - hi.astype(F32)).astype(BF16)
    return hi, lo


def _split3(x):
    hi = x.astype(BF16)
    r1 = x - hi.astype(F32)
    mid = r1.astype(BF16)
    lo = (r1 - mid.astype(F32)).astype(BF16)
    return hi, mid, lo


def _dot_split2(x, w):
    hi, lo = _split2(x)
    return _dot(hi, w) + _dot(lo, w)


def _rms(x, g):
    ms = jnp.mean(x * x, axis=-1, keepdims=True)
    return x * lax.rsqrt(ms + RMS_EPS) * g


def _ffn_kernel(x_ref, o_ref, wo_ref, g_ref, wg_ref, wu_ref, wd_ref, out_ref, x1_ref, hn_ref, acc_ref):
    j = pl.program_id(1)

    @pl.when(j == 0)
    def _():
        x1 = x_ref[...] + _dot(o_ref[...], wo_ref[...])
        x1_ref[...] = x1
        hn_ref[...] = _rms(x1, g_ref[...]).astype(BF16)
        acc_ref[...] = jnp.zeros_like(acc_ref)

    hn = hn_ref[...]
    g = _dot(hn, wg_ref[...])
    u = _dot(hn, wu_ref[...])
    a = (g * jax.nn.sigmoid(g) * u).astype(BF16)
    acc_ref[...] += _dot(a, wd_ref[...])

    @pl.when(j == pl.num_programs(1) - 1)
    def _():
        out_ref[...] = x1_ref[...] + acc_ref[...]


def _ffn_layer(x, o, w_out, g_ffn, w_gu, w_down, *, tm=512, tf=256):
    T, D = x.shape
    F = w_down.shape[0]
    nf = F // tf
    assert T % tm == 0 and F % tf == 0
    return pl.pallas_call(
        _ffn_kernel,
        grid=(T // tm, nf),
        in_specs=[
            pl.BlockSpec((tm, D), lambda i, j: (i, 0)),
            pl.BlockSpec((tm, D), lambda i, j: (i, 0)),
            pl.BlockSpec((D, D), lambda i, j: (0, 0)),
            pl.BlockSpec((1, D), lambda i, j: (0, 0)),
            pl.BlockSpec((D, tf), lambda i, j: (0, j)),
            pl.BlockSpec((D, tf), lambda i, j: (0, j + nf)),
            pl.BlockSpec((tf, D), lambda i, j: (j, 0)),
        ],
        out_specs=pl.BlockSpec((tm, D), lambda i, j: (i, 0)),
        out_shape=jax.ShapeDtypeStruct((T, D), F32),
        scratch_shapes=[pltpu.VMEM((tm, D), F32), pltpu.VMEM((tm, D), BF16), pltpu.VMEM((tm, D), F32)],
        compiler_params=_cparams(("parallel", "arbitrary")),
        name="ffn",
    )(x, o, w_out, g_ffn, w_gu, w_gu, w_down)


def _norm_proj_kernel(x_ref, g_ref, w_ref, out_ref, hn_ref):
    @pl.when(pl.program_id(1) == 0)
    def _():
        hn_ref[...] = _rms(x_ref[...], g_ref[...]).astype(BF16)

    out_ref[...] = _dot(hn_ref[...], w_ref[...]).astype(out_ref.dtype)


def _norm_proj(x, g, w, *, tm=512, tn=1024):
    T, D = x.shape
    N = w.shape[1]
    return pl.pallas_call(
        _norm_proj_kernel,
        grid=(T // tm, N // tn),
        in_specs=[
            pl.BlockSpec((tm, D), lambda i, j: (i, 0)),
            pl.BlockSpec((1, D), lambda i, j: (0, 0)),
            pl.BlockSpec((D, tn), lambda i, j: (0, j)),
        ],
        out_specs=pl.BlockSpec((tm, tn), lambda i, j: (i, j)),
        out_shape=jax.ShapeDtypeStruct((T, N), BF16),
        scratch_shapes=[pltpu.VMEM((tm, D), BF16)],
        compiler_params=_cparams(("parallel", "arbitrary")),
        name="norm_proj",
    )(x, g, w)


def _sb_kernel(q_ref, k_ref, v_ref, suo_ref, o_ref, c_ref, a_ref, *, tq, tk):
    qi = pl.program_id(2)
    lane = lax.broadcasted_iota(I32, (1, LANES), 1)
    q = q_ref[...] * jnp.asarray(HEAD_DIM ** -0.5, BF16)
    zero = jnp.zeros_like(q)
    qh = (jnp.where(lane < HEAD_DIM, q, zero), jnp.where(lane >= HEAD_DIM, q, zero))
    suo = suo_ref[...]
    c_ref[...] = jnp.zeros_like(c_ref)
    a_ref[...] = jnp.zeros_like(a_ref)
    qpos = qi * tq + lax.broadcasted_iota(I32, (tq, 1), 0)

    def tile(j, masked):
        ks = pl.multiple_of(j * tk, tk)
        kb = k_ref[pl.ds(ks, tk), :]
        vb = v_ref[pl.ds(ks, tk), :]
        if masked:
            earlier = (ks + lax.broadcasted_iota(I32, (1, tk), 1)) < qpos
        for h in range(2):
            z = _dot_nt(qh[h], kb)
            lm = -(jnp.maximum(z, 0.0) + jnp.log1p(jnp.exp(-jnp.abs(z))))
            if masked:
                lm = jnp.where(earlier, lm, 0.0)
            hi, lo = _split2(lm)
            cs = _dot(hi, suo) + _dot(lo, suo)
            la = z + lm + cs[:, :tk] + c_ref[h]
            att = jnp.exp(la)
            if masked:
                att = jnp.where(earlier, att, 0.0)
            a_ref[h] += _dot(att.astype(BF16), vb)
            c_ref[h] += cs[:, tk:]

    nd = tq // tk
    for t in range(nd):
        tile(qi * nd + (nd - 1 - t), True)

    def body(t, _):
        tile(qi * nd - 1 - t, False)
        return 0

    lax.fori_loop(0, qi * nd, body, 0)
    o_ref[...] = jnp.where(lane < HEAD_DIM, a_ref[0], a_ref[1]).astype(o_ref.dtype)


def _sb_attention(qkv, B, S, *, tq=256, tk=128):
    D = D_MODEL
    r = lax.broadcasted_iota(I32, (tk, 2 * tk), 0)
    c = lax.broadcasted_iota(I32, (tk, 2 * tk), 1)
    suo = jnp.where((c >= tk) | (r > c), 1.0, 0.0).astype(BF16)
    kern = functools.partial(_sb_kernel, tq=tq, tk=tk)
    return pl.pallas_call(
        kern,
        grid=(B, N_PAIRS, S // tq),
        in_specs=[
            pl.BlockSpec((None, tq, LANES), lambda b, p, i: (b, i, p)),
            pl.BlockSpec((None, S, LANES), lambda b, p, i: (b, 0, N_PAIRS + p)),
            pl.BlockSpec((None, S, LANES), lambda b, p, i: (b, 0, 2 * N_PAIRS + p)),
            pl.BlockSpec((tk, 2 * tk), lambda b, p, i: (0, 0)),
        ],
        out_specs=pl.BlockSpec((None, tq, LANES), lambda b, p, i: (b, i, p)),
        out_shape=jax.ShapeDtypeStruct((B, S, D), BF16),
        scratch_shapes=[pltpu.VMEM((2, tq, tk), F32), pltpu.VMEM((2, tq, LANES), F32)],
        compiler_params=_cparams(("parallel", "parallel", "arbitrary")),
        name="sb_attn",
    )(qkv, qkv, qkv, suo)


def _rope_slabs(y, cos, sin):
    lane = lax.broadcasted_iota(I32, (1, LANES), 1)
    first = (lane % HEAD_DIM) < (HEAD_DIM // 2)
    outs = []
    for s in range(y.shape[1] // LANES):
        ys = y[:, s * LANES:(s + 1) * LANES]
        rot = jnp.where(first, pltpu.roll(ys, LANES - HEAD_DIM // 2, axis=1), pltpu.roll(ys, HEAD_DIM // 2, axis=1))
        outs.append(ys * cos[:, s * LANES:(s + 1) * LANES] + rot * sin[:, s * LANES:(s + 1) * LANES])
    return jnp.concatenate(outs, axis=1)


def _dsa_proj_kernel(x_ref, g_ref, w_ref, wih_ref, wil_ref, hp_ref, qn_ref, kn_ref, cos_ref, sin_ref,
                     qkv_ref, qih_ref, qil_ref, kih_ref, kil_ref, wi_ref, hh_ref, hl_ref, *, wi_scale):
    j = pl.program_id(1)
    nq = DS_IDX_HEADS * DS_IDX_DIM

    @pl.when(j == 0)
    def _():
        hn = _rms(x_ref[...], g_ref[...])
        hh, hl = _split2(hn)
        hh_ref[...] = hh
        hl_ref[...] = hl
        wih = wih_ref[...]
        idx = _dot(hh, wih) + _dot(hl, wih) + _dot(hh, wil_ref[...])
        cos = cos_ref[...]
        sin = sin_ref[...]
        qi = _rope_slabs(idx[:, :nq], cos[:, :nq], sin[:, :nq])
        qh, ql = _split2(qi)
        qih_ref[...] = qh
        qil_ref[...] = ql
        tail = idx[:, nq:nq + LANES]
        lane = lax.broadcasted_iota(I32, (1, LANES), 1)
        ki = jnp.where(lane < DS_IDX_DIM, _rope_slabs(tail, cos[:, :LANES], sin[:, :LANES]), 0.0)
        ki = ki + pltpu.roll(ki, DS_IDX_DIM, axis=1)
        kh, kl = _split2(ki)
        kih_ref[...] = kh
        kil_ref[...] = kl
        wi_ref[...] = tail * wi_scale

    y = _dot(hh_ref[...], w_ref[...])

    def normed(gn_ref, scale):
        ms = _dot_split2(y * y, hp_ref[...]) * (1.0 / HEAD_DIM)
        yn = y * lax.rsqrt(ms + RMS_EPS) * gn_ref[...]
        out = _rope_slabs(yn, cos_ref[...], sin_ref[...])
        return (out * scale).astype(qkv_ref.dtype)

    @pl.when(j == 0)
    def _():
        qkv_ref[...] = normed(qn_ref, HEAD_DIM ** -0.5)

    @pl.when(j == 1)
    def _():
        qkv_ref[...] = normed(kn_ref, 1.0)

    @pl.when(j == 2)
    def _():
        qkv_ref[...] = y.astype(qkv_ref.dtype)


def _head_sum_matrix():
    r = lax.broadcasted_iota(I32, (D_MODEL, D_MODEL), 0) // HEAD_DIM
    c = lax.broadcasted_iota(I32, (D_MODEL, D_MODEL), 1) // HEAD_DIM
    return jnp.where(r == c, 1.0, 0.0).astype(BF16)


def _rope_tables(S):
    half = HEAD_DIM // 2
    inv = 1.0 / (ROPE_THETA ** (jnp.arange(half, dtype=F32) / half))
    ang = jnp.arange(S, dtype=F32)[:, None] * inv[None, :]
    cos = jnp.concatenate([jnp.cos(ang), jnp.cos(ang)], axis=1)
    sin = jnp.concatenate([-jnp.sin(ang), jnp.sin(ang)], axis=1)
    return jnp.tile(cos, (1, N_HEADS)), jnp.tile(sin, (1, N_HEADS))


def _dsa_proj(x, g, w_in, q_norm, k_norm, B, S, *, tm=256):
    T, D = x.shape
    c3 = 3 * D
    nq = DS_IDX_HEADS * DS_IDX_DIM
    n_idx = nq + DS_IDX_DIM + DS_IDX_HEADS
    w_qkv = w_in[:, :c3].astype(BF16)
    w_idx = jnp.pad(w_in[:, c3:], ((0, 0), (0, nq + LANES - n_idx)))
    wih = w_idx.astype(BF16)
    wil = (w_idx - wih.astype(F32)).astype(BF16)
    cos, sin = _rope_tables(S)
    qn = jnp.tile(q_norm, N_HEADS)[None, :]
    kn = jnp.tile(k_norm, N_HEADS)[None, :]
    wi_scale = DS_IDX_HEADS ** -0.5 * DS_IDX_DIM ** -0.5
    nsb = S // tm
    row = lambda i, j: (i, 0)
    const = lambda i, j: (0, 0)
    pos = lambda i, j: (i % nsb, 0)
    return pl.pallas_call(
        functools.partial(_dsa_proj_kernel, wi_scale=wi_scale),
        grid=(T // tm, 3),
        in_specs=[
            pl.BlockSpec((tm, D), row),
            pl.BlockSpec((1, D), const),
            pl.BlockSpec((D, D), lambda i, j: (0, j)),
            pl.BlockSpec((D, nq + LANES), const),
            pl.BlockSpec((D, nq + LANES), const),
            pl.BlockSpec((D, D), const),
            pl.BlockSpec((1, D), const),
            pl.BlockSpec((1, D), const),
            pl.BlockSpec((tm, D), pos),
            pl.BlockSpec((tm, D), pos),
        ],
        out_specs=[
            pl.BlockSpec((tm, D), lambda i, j: (i, j)),
            pl.BlockSpec((tm, nq), row),
            pl.BlockSpec((tm, nq), row),
            pl.BlockSpec((tm, LANES), row),
            pl.BlockSpec((tm, LANES), row),
            pl.BlockSpec((tm, LANES), row),
        ],
        out_shape=[
            jax.ShapeDtypeStruct((T, c3), BF16),
            jax.ShapeDtypeStruct((T, nq), BF16),
            jax.ShapeDtypeStruct((T, nq), BF16),
            jax.ShapeDtypeStruct((T, LANES), BF16),
            jax.ShapeDtypeStruct((T, LANES), BF16),
            jax.ShapeDtypeStruct((T, LANES), F32),
        ],
        scratch_shapes=[pltpu.VMEM((tm, D), BF16), pltpu.VMEM((tm, D), BF16)],
        compiler_params=_cparams(("parallel", "arbitrary")),
        name="dsa_proj",
    )(x, g, w_qkv, wih, wil, _head_sum_matrix(), qn, kn, cos, sin)


def _dsa_attn_kernel(q_ref, k_ref, v_ref, qih_ref, qil_ref, kih_ref, kil_ref, wi_ref, tri_ref, o_ref,
                     keys_ref, bias_ref, m_ref, l_ref, acc_ref, *, tq, ta, tc, n_sel):
    i = pl.program_id(1)
    p = pl.program_id(2)
    lane = lax.broadcasted_iota(I32, (1, LANES), 1)
    row = lax.broadcasted_iota(I32, (tq, 1), 0)
    vis_row = i * tq + (row // DS_CHUNK + 1) * DS_CHUNK
    vis_tile = (i + 1) * tq
    n_a = (vis_tile + ta - 1) // ta

    @pl.when(p == 0)
    def _select():
        wi = wi_ref[...]
        qm = []
        for h in range(DS_IDX_HEADS):
            sl = slice((h // 2) * LANES, (h // 2 + 1) * LANES)
            msk = (lane < DS_IDX_DIM) if h % 2 == 0 else (lane >= DS_IDX_DIM)
            zero = jnp.zeros((tq, LANES), BF16)
            qm.append((jnp.where(msk, qih_ref[:, sl], zero), jnp.where(msk, qil_ref[:, sl], zero),
                       wi[:, DS_IDX_DIM + h:DS_IDX_DIM + h + 1]))

        def score_blk(c, _):
            cs = pl.multiple_of(c * ta, ta)
            kh = kih_ref[pl.ds(cs, ta), :]
            kl = kil_ref[pl.ds(cs, ta), :]
            sc = jnp.zeros((tq, ta), F32)
            for qh, ql, w in qm:
                idx = _dot_nt(qh, kh) + _dot_nt(ql, kh) + _dot_nt(qh, kl)
                sc = sc + w * jnp.maximum(idx, 0.0)
            sc = jnp.where(sc == 0.0, 0.0, sc)
            kpos = cs + lax.broadcasted_iota(I32, (1, ta), 1)
            sc = jnp.where(kpos < vis_row, sc, -jnp.inf)
            bits = lax.bitcast_convert_type(sc, I32)
            keys_ref[:, pl.ds(cs, ta)] = jnp.where(bits < 0, bits ^ jnp.int32(0x7FFFFFFF), bits)
            return 0

        lax.fori_loop(0, n_a, score_blk, 0)

        def count(pred_fn):
            def blk(c, part):
                cs = pl.multiple_of(c * ta, ta)
                for s in range(ta // LANES):
                    kk = keys_ref[:, pl.ds(cs + s * LANES, LANES)]
                    part = part + jnp.where(pred_fn(kk), 1, 0).astype(I32)
                return part
            part = lax.fori_loop(0, n_a, blk, jnp.zeros((tq, LANES), I32))
            return jnp.sum(part, axis=1, keepdims=True)

        def bit_step(t, thr):
            cand = thr + lax.shift_left(jnp.int32(1), 31 - t)
            cnt = count(lambda kk: kk >= cand)
            return jnp.where(cnt >= n_sel, cand, thr)

        thr = lax.fori_loop(0, 32, bit_step, jnp.full((tq, 1), jnp.iinfo(jnp.int32).min, I32))
        need = (n_sel - count(lambda kk: kk > thr)).astype(F32)
        tri = tri_ref[...]

        def bias_blk(c, eq_before):
            cs = pl.multiple_of(c * ta, ta)
            kk = keys_ref[:, pl.ds(cs, ta)]
            eqf = jnp.where(kk == thr, 1.0, 0.0)
            rank = eq_before + _dot(eqf.astype(BF16), tri)
            sel = jnp.where(kk > thr, 1.0, jnp.where(rank < need, eqf, 0.0))
            kpos = cs + lax.broadcasted_iota(I32, (1, ta), 1)
            sel = jnp.where(kpos < vis_row, sel, 0.0)
            bias_ref[:, pl.ds(cs, ta)] = (sel - 1.0) * (-NEG_BIG)
            return eq_before + jnp.sum(eqf, axis=1, keepdims=True)

        lax.fori_loop(0, n_a, bias_blk, jnp.zeros((tq, 1), F32))

    q = q_ref[...]
    zero = jnp.zeros_like(q)
    qh2 = (jnp.where(lane < HEAD_DIM, q, zero), jnp.where(lane >= HEAD_DIM, q, zero))
    m_ref[...] = jnp.full(m_ref.shape, NEG_BIG, F32)
    l_ref[...] = jnp.zeros_like(l_ref)
    acc_ref[...] = jnp.zeros_like(acc_ref)

    def attn_blk(c, _):
        cs = pl.multiple_of(c * tc, tc)
        kb = k_ref[pl.ds(cs, tc), :]
        vb = v_ref[pl.ds(cs, tc), :]
        bias = bias_ref[:, pl.ds(cs, tc)]
        pv = []
        al = []
        for h in range(2):
            s = _dot_nt(qh2[h], kb) + bias
            m_old = m_ref[h]
            m_new = jnp.maximum(m_old, jnp.max(s, axis=1, keepdims=True))
            pr = jnp.exp(s - m_new[:, :1])
            alpha = jnp.exp(m_old - m_new)
            l_ref[h] = alpha * l_ref[h] + jnp.sum(pr, axis=1, keepdims=True)
            m_ref[h] = m_new
            pv.append(_dot(pr.astype(BF16), vb))
            al.append(alpha)
        first = lane < HEAD_DIM
        acc_ref[...] = acc_ref[...] * jnp.where(first, al[0], al[1]) + jnp.where(first, pv[0], pv[1])
        return 0

    lax.fori_loop(0, vis_tile // tc, attn_blk, 0)
    den = jnp.where(lane < HEAD_DIM, l_ref[0], l_ref[1])
    o_ref[...] = (acc_ref[...] / den).astype(o_ref.dtype)


def _dsa_attention(qkv, qih, qil, kih, kil, wi, B, S, *, tq=256, ta=512, tc=256):
    D = D_MODEL
    n_sel = min(DS_TOPK_MAX, S // 4)
    ta = min(ta, S)
    nq = DS_IDX_HEADS * DS_IDX_DIM
    r = lax.broadcasted_iota(I32, (ta, ta), 0)
    c = lax.broadcasted_iota(I32, (ta, ta), 1)
    tri = jnp.where(r < c, 1.0, 0.0).astype(BF16)
    kern = functools.partial(_dsa_attn_kernel, tq=tq, ta=ta, tc=tc, n_sel=n_sel)
    s_pad = -(-S // ta) * ta
    return pl.pallas_call(
        kern,
        grid=(B, S // tq, N_PAIRS),
        in_specs=[
            pl.BlockSpec((None, tq, LANES), lambda b, i, p: (b, i, p)),
            pl.BlockSpec((None, S, LANES), lambda b, i, p: (b, 0, N_PAIRS + p)),
            pl.BlockSpec((None, S, LANES), lambda b, i, p: (b, 0, 2 * N_PAIRS + p)),
            pl.BlockSpec((None, tq, nq), lambda b, i, p: (b, i, 0)),
            pl.BlockSpec((None, tq, nq), lambda b, i, p: (b, i, 0)),
            pl.BlockSpec((None, S, LANES), lambda b, i, p: (b, 0, 0)),
            pl.BlockSpec((None, S, LANES), lambda b, i, p: (b, 0, 0)),
            pl.BlockSpec((None, tq, LANES), lambda b, i, p: (b, i, 0)),
            pl.BlockSpec((ta, ta), lambda b, i, p: (0, 0)),
        ],
        out_specs=pl.BlockSpec((None, tq, LANES), lambda b, i, p: (b, i, p)),
        out_shape=jax.ShapeDtypeStruct((B, S, D), BF16),
        scratch_shapes=[
            pltpu.VMEM((tq, s_pad), I32),
            pltpu.VMEM((tq, s_pad), F32),
            pltpu.VMEM((2, tq, LANES), F32),
            pltpu.VMEM((2, tq, LANES), F32),
            pltpu.VMEM((tq, LANES), F32),
        ],
        compiler_params=_cparams(("parallel", "arbitrary", "arbitrary")),
        name="dsa_attn",
    )(qkv, qkv, qkv, qih.reshape(B, S, nq), qil.reshape(B, S, nq), kih.reshape(B, S, LANES),
      kil.reshape(B, S, LANES), wi.reshape(B, S, LANES), tri)


def _rw_pre_kernel(*refs, has_vres, seq_tiles):
    if has_vres:
        (x_ref, xp_ref, gn_ref, mix_ref, wr_ref, wk_ref, wv_ref, w0_ref, w1_ref, w2_ref, a0_ref, a1_ref, a2_ref,
         g1_ref, g2_ref, kk_ref, ka_ref, hp_ref, vf_ref, v0_ref, v1_ref, v2_ref,
         r_out, lw_out, k_out, v_out, kn_out, b_out, g_out) = refs
    else:
        (x_ref, xp_ref, gn_ref, mix_ref, wr_ref, wk_ref, wv_ref, w0_ref, w1_ref, w2_ref, a0_ref, a1_ref, a2_ref,
         g1_ref, g2_ref, kk_ref, ka_ref, hp_ref,
         r_out, lw_out, k_out, v_out, kn_out, b_out, g_out) = refs
    i = pl.program_id(0)
    gn = gn_ref[...]
    h = _rms(x_ref[...], gn)
    tm = h.shape[0]
    hprev = _rms(xp_ref[...], gn)[7:8, :]
    hprev = jnp.where(i % seq_tiles == 0, 0.0, hprev)
    rowi = lax.broadcasted_iota(I32, (tm, 1), 0)
    sh = jnp.where(rowi == 0, hprev, pltpu.roll(h, 1, axis=0))
    dlt = sh - h
    mix = mix_ref[...]

    def stream(j):
        return (h + dlt * mix[j:j + 1, :]).astype(BF16)

    r = _dot(stream(0), wr_ref[...])
    k = _dot(stream(1), wk_ref[...])
    xv = stream(2)
    v = _dot(xv, wv_ref[...])
    w = w0_ref[...] + _dot(jnp.tanh(_dot(stream(3), w1_ref[...])).astype(BF16), w2_ref[...])
    a = jax.nn.sigmoid(a0_ref[...] + _dot(_dot(stream(4), a1_ref[...]).astype(BF16), a2_ref[...]))
    g = _dot(jax.nn.sigmoid(_dot(stream(5), g1_ref[...])).astype(BF16), g2_ref[...])
    if has_vres:
        gate = jax.nn.sigmoid(v0_ref[...] + _dot(_dot(xv, v1_ref[...]).astype(BF16), v2_ref[...]))
        v = v + (vf_ref[...] - v) * gate
    logw = -(jnp.maximum(-w, 0.0) + jnp.log1p(jnp.exp(-jnp.abs(w)))) - 0.5
    lw_out[...] = -jnp.exp(logw)
    kk = k * kk_ref[...]
    ss = _dot_split2(kk * kk, hp_ref[...])
    kk = kk * lax.rsqrt(jnp.maximum(ss, 1e-24))
    r_out[...] = r
    k_out[...] = k * (1.0 + (a - 1.0) * ka_ref[...])
    v_out[...] = v
    kn_out[...] = kk
    b_out[...] = kk * a
    g_out[...] = g


def _pad_cols(w, n):
    return jnp.pad(w, ((0, 0), (0, n - w.shape[1])))


def _pad_rows(w, n):
    return jnp.pad(w, ((0, n - w.shape[0]), (0, 0)))


def _rw_pre(x, gn, mix, w_rkv, w0, w1, w2, a0, a1, a2, g1, g2, k_k, k_a, v_first, v_res, S, *, tm=256):
    T, D = x.shape
    has_vres = v_res is not None
    row = lambda i: (i, 0)
    const = lambda i: (0, 0)

    def lora(wa, wb):
        n = -(-wa.shape[1] // LANES) * LANES
        return _pad_cols(wa, n).astype(BF16), _pad_rows(wb, n).astype(BF16)

    w1p, w2p = lora(w1, w2)
    a1p, a2p = lora(a1, a2)
    g1p, g2p = lora(g1, g2)
    args = [x, x, gn, mix, w_rkv[0].astype(BF16), w_rkv[1].astype(BF16), w_rkv[2].astype(BF16),
            w0[None, :], w1p, w2p, a0[None, :], a1p, a2p, g1p, g2p, k_k[None, :], k_a[None, :], _head_sum_matrix()]
    full = lambda a: pl.BlockSpec(a.shape, const)
    in_specs = [pl.BlockSpec((tm, D), row),
                pl.BlockSpec((8, D), lambda i: (jnp.maximum(i * (tm // 8) - 1, 0), 0))]
    in_specs += [full(a) for a in args[2:]]
    if has_vres:
        v0, v1, v2 = v_res
        v1p, v2p = lora(v1, v2)
        extra = [v_first, v0[None, :], v1p, v2p]
        args += extra
        in_specs += [pl.BlockSpec((tm, D), row)] + [full(a) for a in extra[1:]]
    outs = pl.pallas_call(
        functools.partial(_rw_pre_kernel, has_vres=has_vres, seq_tiles=S // tm),
        grid=(T // tm,),
        in_specs=in_specs,
        out_specs=[pl.BlockSpec((tm, D), row)] * 7,
        out_shape=[jax.ShapeDtypeStruct((T, D), F32)] * 7,
        compiler_params=_cparams(("parallel",)),
        name="rw_pre",
    )(*args)
    return outs


def _rw_scan_kernel(r_ref, lw_ref, k_ref, v_ref, kn_ref, b_ref, y_ref, s_ref):
    L = RW_CHUNK

    @pl.when(pl.program_id(1) == 0)
    def _():
        s_ref[...] = jnp.zeros_like(s_ref)

    lw = lw_ref[...]
    tr = lax.broadcasted_iota(I32, (L, L), 0)
    tc = lax.broadcasted_iota(I32, (L, L), 1)
    tri = jnp.where(tr >= tc, 1.0, 0.0).astype(BF16)
    hi, mid, lo = _split3(lw)
    lwc = _dot(tri, hi) + _dot(tri, mid) + _dot(tri, lo)
    wl = lwc[L - 1:L, :]
    w_inc = jnp.exp(lwc)
    w_exc = jnp.exp(lwc - lw)
    w_inv = jnp.exp(-lwc)
    w_end = jnp.exp(wl - lwc)
    w_all = jnp.exp(wl)
    kv = k_ref[...]
    bv = b_ref[...]
    vv = v_ref[...]
    rt = r_ref[...] * w_inc
    at = -kn_ref[...] * w_exc
    kt = kv * w_inv
    bt = bv * w_inv
    ke = kv * w_end
    be = bv * w_end

    lane = lax.broadcasted_iota(I32, (1, LANES), 1)
    first = lane < HEAD_DIM

    def sm(x):
        return jnp.concatenate([jnp.where(first, x, 0.0), jnp.where(first, 0.0, x)], axis=0)

    n2 = 2 * L
    ri = lax.broadcasted_iota(I32, (2 * n2, 2 * n2), 0)
    ci = lax.broadcasted_iota(I32, (2 * n2, 2 * n2), 1)
    causal = (ri % L) + ri // n2 > (ci % L)
    bi = lax.broadcasted_iota(I32, (n2, n2), 0) // L
    bj = lax.broadcasted_iota(I32, (n2, n2), 1) // L
    blockdiag = bi == bj
    eye = jnp.where(lax.broadcasted_iota(I32, (n2, n2), 0) == lax.broadcasted_iota(I32, (n2, n2), 1), 1.0, 0.0)

    for p in range(N_PAIRS):
        sl = slice(p * LANES, (p + 1) * LANES)
        vs = sm(vv[:, sl])
        a_l = jnp.concatenate([sm(at[:, sl]), sm(rt[:, sl])], axis=0).astype(BF16)
        a_r = jnp.concatenate([sm(bt[:, sl]), sm(kt[:, sl])], axis=0).astype(BF16)
        m = jnp.where(causal, _dot_nt(a_l, a_r), 0.0)
        a_ab = m[:n2, :n2]
        pm = a_ab.astype(BF16)
        tm = eye + a_ab
        for _ in range(5):
            pm = _dot(pm, pm).astype(BF16)
            tm = tm + _dot(tm.astype(BF16), pm)
        s_old = s_ref[p]
        p12 = _dot_nt(a_l, s_old.astype(BF16))
        vsb = vs.astype(BF16)
        z = p12[:n2] + _dot(m[:n2, n2:].astype(BF16), vsb)
        u = _dot(tm.astype(BF16), z.astype(BF16))
        y = p12[n2:] + _dot(m[n2:, :].astype(BF16), jnp.concatenate([u.astype(BF16), vsb], axis=0))
        y_ref[:, sl] = y[:L] + y[L:]
        us = u[:L] + u[L:]
        upd = _dot_tn(jnp.concatenate([us, vv[:, sl]], axis=0).astype(BF16),
                      jnp.concatenate([be[:, sl], ke[:, sl]], axis=0).astype(BF16))
        s_ref[p] = s_old * w_all[:, sl] + jnp.where(blockdiag, upd, 0.0)


def _rw_scan(r, lw, k, v, kn, b, B, S):
    D = D_MODEL
    L = RW_CHUNK
    spec = pl.BlockSpec((None, L, D), lambda bb, c: (bb, c, 0))
    args = [a.reshape(B, S, D) for a in (r, lw, k, v, kn, b)]
    y = pl.pallas_call(
        _rw_scan_kernel,
        grid=(B, S // L),
        in_specs=[spec] * 6,
        out_specs=spec,
        out_shape=jax.ShapeDtypeStruct((B, S, D), F32),
        scratch_shapes=[pltpu.VMEM((N_PAIRS, LANES, LANES), F32)],
        compiler_params=_cparams(("parallel", "arbitrary")),
        name="rw_scan",
    )(*args)
    return y.reshape(B * S, D)


def _rw_post_kernel(y_ref, r_ref, k_ref, v_ref, g_ref, rk_ref, lnw_ref, lnb_ref, hp_ref, o_ref):
    hp = hp_ref[...]
    y = y_ref[...]
    inv = 1.0 / HEAD_DIM
    mu = _dot_split2(y, hp) * inv
    yc = y - mu
    var = _dot_split2(yc * yc, hp) * inv
    yn = yc * lax.rsqrt(var + RW_GN_EPS) * lnw_ref[...] + lnb_ref[...]
    bonus = _dot_split2(r_ref[...] * k_ref[...] * rk_ref[...], hp) * v_ref[...]
    o_ref[...] = ((yn + bonus) * g_ref[...]).astype(o_ref.dtype)


def _rw_post(y, r, k, v, g, r_k, ln_w, ln_b, *, tm=256):
    T, D = y.shape
    row = lambda i: (i, 0)
    const = lambda i: (0, 0)
    return pl.pallas_call(
        _rw_post_kernel,
        grid=(T // tm,),
        in_specs=[pl.BlockSpec((tm, D), row)] * 5 + [pl.BlockSpec((1, D), const)] * 3 + [pl.BlockSpec((D, D), const)],
        out_specs=pl.BlockSpec((tm, D), row),
        out_shape=jax.ShapeDtypeStruct((T, D), BF16),
        compiler_params=_cparams(("parallel",)),
        name="rw_post",
    )(y, r, k, v, g, r_k.reshape(1, D), ln_w[None, :], ln_b[None, :], _head_sum_matrix())


def _rwkv_mixer(x, gn, B, S, mix, w_rkv, w0, w1, w2, a0, a1, a2, g1, g2, k_k, k_a, r_k, ln_w, ln_b, v_first, v_res):
    r, lw, k, v, kn, b, g = _rw_pre(x, gn, mix, w_rkv, w0, w1, w2, a0, a1, a2, g1, g2, k_k, k_a, v_first, v_res, S)
    y = _rw_scan(r, lw, k, v, kn, b, B, S)
    return _rw_post(y, r, k, v, g, r_k, ln_w, ln_b), v


def _sb_mixer(x, gn, B, S, w_qkv):
    qkv = _norm_proj(x, gn, w_qkv.astype(BF16))
    return _sb_attention(qkv.reshape(B, S, 3 * D_MODEL), B, S).reshape(B * S, D_MODEL)


def _dsa_mixer(x, gn, B, S, w_in, q_norm, k_norm):
    qkv, qih, qil, kih, kil, wi = _dsa_proj(x, gn, w_in, q_norm, k_norm, B, S)
    o = _dsa_attention(qkv.reshape(B, S, 3 * D_MODEL), qih, qil, kih, kil, wi, B, S)
    return o.reshape(B * S, D_MODEL)


def kernel(x, norm_mix, norm_ffn, ffn_w_gu, ffn_w_down, rw_mix, rw_w_rkv, rw_w0, rw_w1, rw_w2, rw_a0, rw_a1, rw_a2, rw_g1, rw_g2, rw_v0, rw_v1, rw_v2, rw_k_k, rw_k_a, rw_r_k, rw_ln_w, rw_ln_b, rw_w_out, sb_w_qkv, sb_w_out, ds_w_in, ds_q_norm, ds_k_norm, ds_w_out):
    B, S, D = x.shape
    depth = norm_mix.shape[0]
    xf = x.reshape(B * S, D)
    v_first = None
    for i in range(depth):
        kind, j = i % 3, i // 3
        gn = norm_mix[i][None, :]
        if kind == 0:
            v_res = None if j == 0 else (rw_v0[j - 1], rw_v1[j - 1], rw_v2[j - 1])
            o, v_layer = _rwkv_mixer(xf, gn, B, S, rw_mix[j], rw_w_rkv[j], rw_w0[j], rw_w1[j], rw_w2[j],
                                     rw_a0[j], rw_a1[j], rw_a2[j], rw_g1[j], rw_g2[j], rw_k_k[j], rw_k_a[j],
                                     rw_r_k[j], rw_ln_w[j], rw_ln_b[j], v_first, v_res)
            if j == 0:
                v_first = v_layer
            w_out = rw_w_out[j]
        elif kind == 1:
            o = _sb_mixer(xf, gn, B, S, sb_w_qkv[j])
            w_out = sb_w_out[j]
        else:
            o = _dsa_mixer(xf, gn, B, S, ds_w_in[j], ds_q_norm[j], ds_k_norm[j])
            w_out = ds_w_out[j]
        xf = _ffn_layer(xf, o, w_out.astype(BF16), norm_ffn[i][None, :], ffn_w_gu[i].astype(BF16),
                        ffn_w_down[i].astype(BF16))
    return xf.reshape(B, S, D)
```

```python
import functools
import math

import jax
import jax.numpy as jnp
from jax import lax
from jax.experimental import pallas as pl
from jax.experimental.pallas import tpu as pltpu

F32 = jnp.float32
BF16 = jnp.bfloat16
I32 = jnp.int32

D_MODEL = 1024
HEAD_DIM = 64
N_HEADS = D_MODEL // HEAD_DIM
N_PAIRS = N_HEADS // 2
LANES = 128
RMS_EPS = 1e-6
ROPE_THETA = 10000.0
RW_GN_EPS = HEAD_DIM * 1e-5
RW_CHUNK = 64
DS_TOPK_MAX = 256
DS_IDX_HEADS = 8
DS_IDX_DIM = 64
DS_CHUNK = 64
NEG_BIG = -1e30
VMEM_LIMIT = 56 * 1024 * 1024


def _cparams(sem):
    return pltpu.CompilerParams(dimension_semantics=sem, vmem_limit_bytes=VMEM_LIMIT)


def _dot(a, b):
    return jnp.dot(a, b, preferred_element_type=F32)


def _dot_nt(a, b):
    return lax.dot_general(a, b, (((1,), (1,)), ((), ())), preferred_element_type=F32)


def _dot_tn(a, b):
    return lax.dot_general(a, b, (((0,), (0,)), ((), ())), preferred_element_type=F32)


def _split2(x):
    hi = x.astype(BF16)
    lo = (x - hi.astype(F32)).astype(BF16)
    return hi, lo


def _split3(x):
    hi = x.astype(BF16)
    r1 = x - hi.astype(F32)
    mid = r1.astype(BF16)
    lo = (r1 - mid.astype(F32)).astype(BF16)
    return hi, mid, lo


def _dot_split2(x, w):
    hi, lo = _split2(x)
    return _dot(hi, w) + _dot(lo, w)


def _rms(x, g):
    ms = jnp.mean(x * x, axis=-1, keepdims=True)
    return x * lax.rsqrt(ms + RMS_EPS) * g


def _ffn_kernel(x_ref, o_ref, wo_ref, g_ref, wg_ref, wu_ref, wd_ref, out_ref, x1_ref, hn_ref, acc_ref):
    j = pl.program_id(1)

    @pl.when(j == 0)
    def _():
        x1 = x_ref[...] + _dot(o_ref[...], wo_ref[...])
        x1_ref[...] = x1
        hn_ref[...] = _rms(x1, g_ref[...]).astype(BF16)
        acc_ref[...] = jnp.zeros_like(acc_ref)

    hn = hn_ref[...]
    g = _dot(hn, wg_ref[...])
    u = _dot(hn, wu_ref[...])
    a = (g * jax.nn.sigmoid(g) * u).astype(BF16)
    acc_ref[...] += _dot(a, wd_ref[...])

    @pl.when(j == pl.num_programs(1) - 1)
    def _():
        out_ref[...] = x1_ref[...] + acc_ref[...]


def _ffn_layer(x, o, w_out, g_ffn, w_gu, w_down, *, tm=512, tf=256):
    T, D = x.shape
    F = w_down.shape[0]
    nf = F // tf
    assert T % tm == 0 and F % tf == 0
    return pl.pallas_call(
        _ffn_kernel,
        grid=(T // tm, nf),
        in_specs=[
            pl.BlockSpec((tm, D), lambda i, j: (i, 0)),
            pl.BlockSpec((tm, D), lambda i, j: (i, 0)),
            pl.BlockSpec((D, D), lambda i, j: (0, 0)),
            pl.BlockSpec((1, D), lambda i, j: (0, 0)),
            pl.BlockSpec((D, tf), lambda i, j: (0, j)),
            pl.BlockSpec((D, tf), lambda i, j: (0, j + nf)),
            pl.BlockSpec((tf, D), lambda i, j: (j, 0)),
        ],
        out_specs=pl.BlockSpec((tm, D), lambda i, j: (i, 0)),
        out_shape=jax.ShapeDtypeStruct((T, D), F32),
        scratch_shapes=[pltpu.VMEM((tm, D), F32), pltpu.VMEM((tm, D), BF16), pltpu.VMEM((tm, D), F32)],
        compiler_params=_cparams(("parallel", "arbitrary")),
        name="ffn",
    )(x, o, w_out, g_ffn, w_gu, w_gu, w_down)


def _norm_proj_kernel(x_ref, g_ref, w_ref, out_ref, hn_ref):
    @pl.when(pl.program_id(1) == 0)
    def _():
        hn_ref[...] = _rms(x_ref[...], g_ref[...]).astype(BF16)

    out_ref[...] = _dot(hn_ref[...], w_ref[...]).astype(out_ref.dtype)


def _norm_proj(x, g, w, *, tm=512, tn=1024):
    T, D = x.shape
    N = w.shape[1]
    return pl.pallas_call(
        _norm_proj_kernel,
        grid=(T // tm, N // tn),
        in_specs=[
            pl.BlockSpec((tm, D), lambda i, j: (i, 0)),
            pl.BlockSpec((1, D), lambda i, j: (0, 0)),
            pl.BlockSpec((D, tn), lambda i, j: (0, j)),
        ],
        out_specs=pl.BlockSpec((tm, tn), lambda i, j: (i, j)),
        out_shape=jax.ShapeDtypeStruct((T, N), BF16),
        scratch_shapes=[pltpu.VMEM((tm, D), BF16)],
        compiler_params=_cparams(("parallel", "arbitrary")),
        name="norm_proj",
    )(x, g, w)


def _sb_kernel(q_ref, k_ref, v_ref, suo_ref, o_ref, c_ref, a_ref, zs_ref, hl_ref, *, tq, tk):
    qi = pl.program_id(2)
    lane = lax.broadcasted_iota(I32, (1, LANES), 1)
    first = lane < HEAD_DIM
    q = q_ref[...] * jnp.asarray(HEAD_DIM ** -0.5, BF16)
    zero = jnp.zeros_like(q)
    qh = (jnp.where(first, q, zero), jnp.where(first, zero, q))
    suo = suo_ref[...]
    c_ref[...] = jnp.zeros_like(c_ref)
    a_ref[...] = jnp.zeros_like(a_ref)
    qpos = qi * tq + lax.broadcasted_iota(I32, (tq, 1), 0)
    nsub = tq // tk

    def stage_a(j, slot, masked):
        ks = pl.multiple_of(j * tq, tq)
        kb = k_ref[pl.ds(ks, tq), :]
        if masked:
            earlier = (ks + lax.broadcasted_iota(I32, (1, tq), 1)) < qpos
        for h in range(2):
            z = _dot_nt(qh[h], kb)
            sp = jnp.maximum(z, 0.0) + jnp.log(1.0 + jnp.exp(-jnp.abs(z)))
            zs = z - sp
            if masked:
                sp = jnp.where(earlier, sp, 0.0)
                zs = jnp.where(earlier, zs, NEG_BIG)
            zs_ref[slot, h] = zs
            for u in range(nsub):
                hi, lo = _split2(sp[:, u * tk:(u + 1) * tk])
                hl_ref[slot, h, :, 2 * u * tk:(2 * u + 1) * tk] = hi
                hl_ref[slot, h, :, (2 * u + 1) * tk:(2 * u + 2) * tk] = lo

    def stage_b(j, slot):
        ks = pl.multiple_of(j * tq, tq)
        vb = v_ref[pl.ds(ks, tq), :]
        zv = jnp.zeros_like(vb)
        vb2 = jnp.concatenate([jnp.where(first, vb, zv), jnp.where(first, zv, vb)], axis=0)
        att = []
        for h in range(2):
            cs = [_dot(hl_ref[slot, h, :, 2 * u * tk:(2 * u + 2) * tk], suo) for u in range(nsub)]
            c = c_ref[h]
            parts = [None] * nsub
            for u in reversed(range(nsub)):
                a = jnp.exp(zs_ref[slot, h, :, u * tk:(u + 1) * tk] + cs[u][:, :tk] + c)
                parts[u] = a.astype(BF16)
                c = c + cs[u][:, tk:]
            c_ref[h] = c
            att += parts
        a_ref[...] += _dot(jnp.concatenate(att, axis=1), vb2)

    stage_a(qi, 0, True)

    def body(t, _):
        j = qi - 2 * t
        stage_a(j - 1, 1, False)
        stage_b(j, 0)
        stage_a(j - 2, 0, False)
        stage_b(j - 1, 1)
        return 0

    lax.fori_loop(0, qi // 2, body, 0)

    @pl.when(qi % 2 == 1)
    def _():
        stage_a(0, 1, False)
        stage_b(1, 0)
        stage_b(0, 1)

    @pl.when(qi % 2 == 0)
    def _():
        stage_b(0, 0)

    o_ref[...] = a_ref[...].astype(o_ref.dtype)


def _sb_attention(qkv, B, S, *, tq=256, tk=128):
    D = D_MODEL
    r = lax.broadcasted_iota(I32, (tk, 2 * tk), 0)
    c = lax.broadcasted_iota(I32, (tk, 2 * tk), 1)
    suo = jnp.where((c >= tk) | (r > c), -1.0, 0.0).astype(BF16)
    suo = jnp.concatenate([suo, suo], axis=0)
    kern = functools.partial(_sb_kernel, tq=tq, tk=tk)
    return pl.pallas_call(
        kern,
        grid=(B, N_PAIRS, S // tq),
        in_specs=[
            pl.BlockSpec((None, tq, LANES), lambda b, p, i: (b, i, p)),
            pl.BlockSpec((None, S, LANES), lambda b, p, i: (b, 0, N_PAIRS + p)),
            pl.BlockSpec((None, S, LANES), lambda b, p, i: (b, 0, 2 * N_PAIRS + p)),
            pl.BlockSpec((2 * tk, 2 * tk), lambda b, p, i: (0, 0)),
        ],
        out_specs=pl.BlockSpec((None, tq, LANES), lambda b, p, i: (b, i, p)),
        out_shape=jax.ShapeDtypeStruct((B, S, D), BF16),
        scratch_shapes=[pltpu.VMEM((2, tq, tk), F32), pltpu.VMEM((tq, LANES), F32),
                        pltpu.VMEM((2, 2, tq, tq), F32), pltpu.VMEM((2, 2, tq, 2 * tq), BF16)],
        compiler_params=_cparams(("parallel", "parallel", "arbitrary")),
        name="sb_attn",
    )(qkv, qkv, qkv, suo)


def _rope_slabs(y, cos, sin):
    lane = lax.broadcasted_iota(I32, (1, LANES), 1)
    first = (lane % HEAD_DIM) < (HEAD_DIM // 2)
    outs = []
    for s in range(y.shape[1] // LANES):
        ys = y[:, s * LANES:(s + 1) * LANES]
        rot = jnp.where(first, pltpu.roll(ys, LANES - HEAD_DIM // 2, axis=1), pltpu.roll(ys, HEAD_DIM // 2, axis=1))
        outs.append(ys * cos[:, s * LANES:(s + 1) * LANES] + rot * sin[:, s * LANES:(s + 1) * LANES])
    return jnp.concatenate(outs, axis=1)


def _dsa_proj_kernel(x_ref, g_ref, w_ref, wih_ref, wil_ref, hp_ref, qn_ref, kn_ref, cos_ref, sin_ref,
                     qkv_ref, qcat_ref, kcat_ref, wi_ref, hh_ref, hl_ref, *, wi_scale):
    j = pl.program_id(1)
    nq = DS_IDX_HEADS * DS_IDX_DIM

    @pl.when(j == 0)
    def _():
        hn = _rms(x_ref[...], g_ref[...])
        hh, hl = _split2(hn)
        hh_ref[...] = hh
        hl_ref[...] = hl
        wih = wih_ref[...]
        idx = _dot(hh, wih) + _dot(hl, wih) + _dot(hh, wil_ref[...])
        cos = cos_ref[...]
        sin = sin_ref[...]
        lane = lax.broadcasted_iota(I32, (1, LANES), 1)
        first = lane < DS_IDX_DIM

        def hi_lo(x):
            hi = x.astype(BF16).astype(F32)
            return hi, x - hi

        qi = _rope_slabs(idx[:, :nq], cos[:, :nq], sin[:, :nq])
        qhi, qlo = hi_lo(qi)
        for h in range(DS_IDX_HEADS):
            sl = slice((h // 2) * LANES, (h // 2 + 1) * LANES)
            if h % 2 == 0:
                own_hi = qhi[:, sl]
                a = jnp.where(first, own_hi, pltpu.roll(qlo[:, sl], DS_IDX_DIM, axis=1))
            else:
                own_hi = pltpu.roll(qhi[:, sl], DS_IDX_DIM, axis=1)
                a = jnp.where(first, own_hi, qlo[:, sl])
            b = jnp.where(first, own_hi, 0.0)
            qcat_ref[:, 2 * h * LANES:(2 * h + 1) * LANES] = a.astype(BF16)
            qcat_ref[:, (2 * h + 1) * LANES:(2 * h + 2) * LANES] = b.astype(BF16)
        tail = idx[:, nq:nq + LANES]
        ki = jnp.where(first, _rope_slabs(tail, cos[:, :LANES], sin[:, :LANES]), 0.0)
        khi, klo = hi_lo(ki)
        kcat_ref[:, :LANES] = (khi + pltpu.roll(khi, DS_IDX_DIM, axis=1)).astype(BF16)
        kcat_ref[:, LANES:] = klo.astype(BF16)
        wi_ref[...] = tail * wi_scale

    y = _dot(hh_ref[...], w_ref[...])

    def normed(gn_ref, scale):
        ms = _dot_split2(y * y, hp_ref[...]) * (1.0 / HEAD_DIM)
        yn = y * lax.rsqrt(ms + RMS_EPS) * gn_ref[...]
        out = _rope_slabs(yn, cos_ref[...], sin_ref[...])
        return (out * scale).astype(qkv_ref.dtype)

    @pl.when(j == 0)
    def _():
        qkv_ref[...] = normed(qn_ref, HEAD_DIM ** -0.5)

    @pl.when(j == 1)
    def _():
        qkv_ref[...] = normed(kn_ref, 1.0)

    @pl.when(j == 2)
    def _():
        qkv_ref[...] = y.astype(qkv_ref.dtype)


def _head_sum_matrix():
    r = lax.broadcasted_iota(I32, (D_MODEL, D_MODEL), 0) // HEAD_DIM
    c = lax.broadcasted_iota(I32, (D_MODEL, D_MODEL), 1) // HEAD_DIM
    return jnp.where(r == c, 1.0, 0.0).astype(BF16)


def _rope_tables(S):
    half = HEAD_DIM // 2
    inv = 1.0 / (ROPE_THETA ** (jnp.arange(half, dtype=F32) / half))
    ang = jnp.arange(S, dtype=F32)[:, None] * inv[None, :]
    cos = jnp.concatenate([jnp.cos(ang), jnp.cos(ang)], axis=1)
    sin = jnp.concatenate([-jnp.sin(ang), jnp.sin(ang)], axis=1)
    return jnp.tile(cos, (1, N_HEADS)), jnp.tile(sin, (1, N_HEADS))


def _dsa_proj(x, g, w_in, q_norm, k_norm, B, S, *, tm=256):
    T, D = x.shape
    c3 = 3 * D
    nq = DS_IDX_HEADS * DS_IDX_DIM
    n_idx = nq + DS_IDX_DIM + DS_IDX_HEADS
    w_qkv = w_in[:, :c3].astype(BF16)
    w_idx = jnp.pad(w_in[:, c3:], ((0, 0), (0, nq + LANES - n_idx)))
    wih = w_idx.astype(BF16)
    wil = (w_idx - wih.astype(F32)).astype(BF16)
    cos, sin = _rope_tables(S)
    qn = jnp.tile(q_norm, N_HEADS)[None, :]
    kn = jnp.tile(k_norm, N_HEADS)[None, :]
    wi_scale = DS_IDX_HEADS ** -0.5 * DS_IDX_DIM ** -0.5
    nsb = S // tm
    row = lambda i, j: (i, 0)
    const = lambda i, j: (0, 0)
    pos = lambda i, j: (i % nsb, 0)
    return pl.pallas_call(
        functools.partial(_dsa_proj_kernel, wi_scale=wi_scale),
        grid=(T // tm, 3),
        in_specs=[
            pl.BlockSpec((tm, D), row),
            pl.BlockSpec((1, D), const),
            pl.BlockSpec((D, D), lambda i, j: (0, j)),
            pl.BlockSpec((D, nq + LANES), const),
            pl.BlockSpec((D, nq + LANES), const),
            pl.BlockSpec((D, D), const),
            pl.BlockSpec((1, D), const),
            pl.BlockSpec((1, D), const),
            pl.BlockSpec((tm, D), pos),
            pl.BlockSpec((tm, D), pos),
        ],
        out_specs=[
            pl.BlockSpec((tm, D), lambda i, j: (i, j)),
            pl.BlockSpec((tm, 2 * LANES * DS_IDX_HEADS), row),
            pl.BlockSpec((tm, 2 * LANES), row),
            pl.BlockSpec((tm, LANES), row),
        ],
        out_shape=[
            jax.ShapeDtypeStruct((T, c3), BF16),
            jax.ShapeDtypeStruct((T, 2 * LANES * DS_IDX_HEADS), BF16),
            jax.ShapeDtypeStruct((T, 2 * LANES), BF16),
            jax.ShapeDtypeStruct((T, LANES), F32),
        ],
        scratch_shapes=[pltpu.VMEM((tm, D), BF16), pltpu.VMEM((tm, D), BF16)],
        compiler_params=_cparams(("parallel", "arbitrary")),
        name="dsa_proj",
    )(x, g, w_qkv, wih, wil, _head_sum_matrix(), qn, kn, cos, sin)


def _dsa_attn_kernel(q_ref, k_ref, v_ref, qcat_ref, kcat_ref, wi_ref, tri_ref, o_ref,
                     keys_ref, bias_ref, thr_ref, cand_ref, need_ref, eqb_ref, mx_ref, ls_ref, acc_ref,
                     *, tq, ta, n_sel):
    i = pl.program_id(1)
    p = pl.program_id(2)
    lane = lax.broadcasted_iota(I32, (1, LANES), 1)
    row = lax.broadcasted_iota(I32, (tq, 1), 0)
    vis_row = i * tq + (row // DS_CHUNK + 1) * DS_CHUNK
    vis_tile = (i + 1) * tq
    n_a = (vis_tile + ta - 1) // ta

    @pl.when(p == 0)
    def _select():
        wi = wi_ref[...]

        def score_blk(c, _):
            cs = pl.multiple_of(c * ta, ta)
            kc = kcat_ref[pl.ds(cs, ta), :]
            sc = jnp.zeros((tq, ta), F32)
            for h in range(DS_IDX_HEADS):
                idx = _dot_nt(qcat_ref[:, 2 * h * LANES:(2 * h + 2) * LANES], kc)
                sc = sc + wi[:, DS_IDX_DIM + h:DS_IDX_DIM + h + 1] * jnp.maximum(idx, 0.0)
            sc = jnp.where(sc == 0.0, 0.0, sc)
            kpos = cs + lax.broadcasted_iota(I32, (1, ta), 1)
            sc = jnp.where(kpos < vis_row, sc, -jnp.inf)
            bits = lax.bitcast_convert_type(sc, I32)
            keys_ref[:, pl.ds(cs, ta)] = jnp.where(bits < 0, bits ^ jnp.int32(0x7FFFFFFF), bits)
            return 0

        lax.fori_loop(0, n_a, score_blk, 0)

        ones = jnp.ones((LANES, LANES), BF16)
        nrep = ta // LANES

        def count(cmp_ref, strict):
            def blk(c, part):
                cs = pl.multiple_of(c * ta, ta)
                for s in range(nrep):
                    kk = keys_ref[:, pl.ds(cs + s * LANES, LANES)]
                    cmp = cmp_ref[...]
                    hit = (kk > cmp) if strict else (kk >= cmp)
                    part = part + jnp.where(hit, 1, 0).astype(I32)
                return part
            part = lax.fori_loop(0, n_a, blk, jnp.zeros((tq, LANES), I32))
            return _dot(part.astype(F32).astype(BF16), ones)

        thr_ref[...] = jnp.full(thr_ref.shape, jnp.iinfo(jnp.int32).min, I32)

        def bit_step(t, _):
            cand_ref[...] = thr_ref[...] + lax.shift_left(jnp.int32(1), 31 - t)
            cnt = count(cand_ref, False)
            thr_ref[...] = jnp.where(cnt >= n_sel, cand_ref[...], thr_ref[...])
            return 0

        lax.fori_loop(0, 32, bit_step, 0)
        need_ref[...] = n_sel - count(thr_ref, True)
        tri = tri_ref[...]

        def bias_blk(c, _):
            cs = pl.multiple_of(c * ta, ta)
            kk = keys_ref[:, pl.ds(cs, ta)]
            thr = jnp.concatenate([thr_ref[...]] * nrep, axis=1)
            eqf = jnp.where(kk == thr, 1.0, 0.0)
            ranks = _dot(eqf.astype(BF16), tri)
            rank = jnp.concatenate([eqb_ref[...]] * nrep, axis=1) + ranks[:, :ta]
            need = jnp.concatenate([need_ref[...]] * nrep, axis=1)
            sel = jnp.where(kk > thr, 1.0, jnp.where(rank < need, eqf, 0.0))
            kpos = cs + lax.broadcasted_iota(I32, (1, ta), 1)
            sel = jnp.where(kpos < vis_row, sel, 0.0)
            bias_ref[:, pl.ds(cs, ta)] = (sel - 1.0) * (-NEG_BIG)
            eqb_ref[...] += ranks[:, ta:]
            return 0

        eqb_ref[...] = jnp.zeros_like(eqb_ref)
        lax.fori_loop(0, n_a, bias_blk, 0)

        @pl.when(n_a % 2 == 1)
        def _():
            bias_ref[:, pl.ds(pl.multiple_of(n_a * ta, ta), ta)] = jnp.full((tq, ta), NEG_BIG, F32)

    q = q_ref[...]
    zero = jnp.zeros_like(q)
    first = lane < HEAD_DIM
    qh2 = (jnp.where(first, q, zero), jnp.where(first, zero, q))
    mx_ref[...] = jnp.full(mx_ref.shape, NEG_BIG, F32)

    n_pair = (n_a + 1) // 2

    def max_blk(c2, _):
        for u in range(2):
            cs = pl.multiple_of((2 * c2 + u) * ta, ta)
            kb = k_ref[pl.ds(cs, ta), :]
            bias = bias_ref[:, pl.ds(cs, ta)]
            for h in range(2):
                mx_ref[u, h] = jnp.maximum(mx_ref[u, h], _dot_nt(qh2[h], kb) + bias)
        return 0

    lax.fori_loop(0, n_pair, max_blk, 0)
    m = [jnp.max(jnp.maximum(mx_ref[0, h], mx_ref[1, h]), axis=1, keepdims=True) for h in range(2)]
    ls_ref[...] = jnp.zeros_like(ls_ref)
    acc_ref[...] = jnp.zeros_like(acc_ref)

    def attn_blk(c2, _):
        for u in range(2):
            cs = pl.multiple_of((2 * c2 + u) * ta, ta)
            kb = k_ref[pl.ds(cs, ta), :]
            vb = v_ref[pl.ds(cs, ta), :]
            zv = jnp.zeros_like(vb)
            vb2 = jnp.concatenate([jnp.where(first, vb, zv), jnp.where(first, zv, vb)], axis=0)
            bias = bias_ref[:, pl.ds(cs, ta)]
            ps = []
            for h in range(2):
                pr = jnp.exp(_dot_nt(qh2[h], kb) + bias - m[h])
                ls_ref[u, h] += pr
                ps.append(pr.astype(BF16))
            acc_ref[u] += _dot(jnp.concatenate(ps, axis=1), vb2)
        return 0

    lax.fori_loop(0, n_pair, attn_blk, 0)
    den = jnp.where(first, jnp.sum(ls_ref[0, 0] + ls_ref[1, 0], axis=1, keepdims=True),
                    jnp.sum(ls_ref[0, 1] + ls_ref[1, 1], axis=1, keepdims=True))
    o_ref[...] = ((acc_ref[0] + acc_ref[1]) / den).astype(o_ref.dtype)


def _dsa_attention(qkv, qcat, kcat, wi, B, S, *, tq=256, ta=512):
    D = D_MODEL
    n_sel = min(DS_TOPK_MAX, S // 4)
    assert S % (2 * ta) == 0 and S // LANES <= 256
    nqc = 2 * LANES * DS_IDX_HEADS
    r = lax.broadcasted_iota(I32, (ta, ta + LANES), 0)
    c = lax.broadcasted_iota(I32, (ta, ta + LANES), 1)
    tri = jnp.where((r < c) | (c >= ta), 1.0, 0.0).astype(BF16)
    kern = functools.partial(_dsa_attn_kernel, tq=tq, ta=ta, n_sel=n_sel)
    s_pad = S
    return pl.pallas_call(
        kern,
        grid=(B, S // tq, N_PAIRS),
        in_specs=[
            pl.BlockSpec((None, tq, LANES), lambda b, i, p: (b, i, p)),
            pl.BlockSpec((None, S, LANES), lambda b, i, p: (b, 0, N_PAIRS + p)),
            pl.BlockSpec((None, S, LANES), lambda b, i, p: (b, 0, 2 * N_PAIRS + p)),
            pl.BlockSpec((None, tq, nqc), lambda b, i, p: (b, i, 0)),
            pl.BlockSpec((None, S, 2 * LANES), lambda b, i, p: (b, 0, 0)),
            pl.BlockSpec((None, tq, LANES), lambda b, i, p: (b, i, 0)),
            pl.BlockSpec((ta, ta + LANES), lambda b, i, p: (0, 0)),
        ],
        out_specs=pl.BlockSpec((None, tq, LANES), lambda b, i, p: (b, i, p)),
        out_shape=jax.ShapeDtypeStruct((B, S, D), BF16),
        scratch_shapes=[
            pltpu.VMEM((tq, s_pad), I32),
            pltpu.VMEM((tq, s_pad), F32),
            pltpu.VMEM((tq, LANES), I32),
            pltpu.VMEM((tq, LANES), I32),
            pltpu.VMEM((tq, LANES), F32),
            pltpu.VMEM((tq, LANES), F32),
            pltpu.VMEM((2, 2, tq, ta), F32),
            pltpu.VMEM((2, 2, tq, ta), F32),
            pltpu.VMEM((2, tq, LANES), F32),
        ],
        compiler_params=_cparams(("parallel", "arbitrary", "arbitrary")),
        name="dsa_attn",
    )(qkv, qkv, qkv, qcat.reshape(B, S, nqc), kcat.reshape(B, S, 2 * LANES), wi.reshape(B, S, LANES), tri)


def _rw_pre_kernel(*refs, has_vres, seq_tiles):
    if has_vres:
        (x_ref, xp_ref, gn_ref, mix_ref, wr_ref, wk_ref, wv_ref, w0_ref, w1_ref, w2_ref, a0_ref, a1_ref, a2_ref,
         g1_ref, g2_ref, kk_ref, ka_ref, hp_ref, vf_ref, v0_ref, v1_ref, v2_ref,
         r_out, lw_out, k_out, v_out, kn_out, b_out, g_out) = refs
    else:
        (x_ref, xp_ref, gn_ref, mix_ref, wr_ref, wk_ref, wv_ref, w0_ref, w1_ref, w2_ref, a0_ref, a1_ref, a2_ref,
         g1_ref, g2_ref, kk_ref, ka_ref, hp_ref,
         r_out, lw_out, k_out, v_out, kn_out, b_out, g_out) = refs
    i = pl.program_id(0)
    gn = gn_ref[...]
    h = _rms(x_ref[...], gn)
    tm = h.shape[0]
    hprev = _rms(xp_ref[...], gn)[7:8, :]
    hprev = jnp.where(i % seq_tiles == 0, 0.0, hprev)
    rowi = lax.broadcasted_iota(I32, (tm, 1), 0)
    sh = jnp.where(rowi == 0, hprev, pltpu.roll(h, 1, axis=0))
    dlt = sh - h
    mix = mix_ref[...]

    def stream(j):
        return (h + dlt * mix[j:j + 1, :]).astype(BF16)

    r = _dot(stream(0), wr_ref[...])
    k = _dot(stream(1), wk_ref[...])
    xv = stream(2)
    v = _dot(xv, wv_ref[...])
    w = w0_ref[...] + _dot(jnp.tanh(_dot(stream(3), w1_ref[...])).astype(BF16), w2_ref[...])
    a = jax.nn.sigmoid(a0_ref[...] + _dot(_dot(stream(4), a1_ref[...]).astype(BF16), a2_ref[...]))
    g = _dot(jax.nn.sigmoid(_dot(stream(5), g1_ref[...])).astype(BF16), g2_ref[...])
    if has_vres:
        gate = jax.nn.sigmoid(v0_ref[...] + _dot(_dot(xv, v1_ref[...]).astype(BF16), v2_ref[...]))
        v = v + (vf_ref[...] - v) * gate
    logw = -(jnp.maximum(-w, 0.0) + jnp.log1p(jnp.exp(-jnp.abs(w)))) - 0.5
    lw_out[...] = -jnp.exp(logw)
    kk = k * kk_ref[...]
    ss = _dot_split2(kk * kk, hp_ref[...])
    kk = kk * lax.rsqrt(jnp.maximum(ss, 1e-24))
    r_out[...] = r
    k_out[...] = k * (1.0 + (a - 1.0) * ka_ref[...])
    v_out[...] = v
    kn_out[...] = kk
    b_out[...] = kk * a
    g_out[...] = g


def _pad_cols(w, n):
    return jnp.pad(w, ((0, 0), (0, n - w.shape[1])))


def _pad_rows(w, n):
    return jnp.pad(w, ((0, n - w.shape[0]), (0, 0)))


def _rw_pre(x, gn, mix, w_rkv, w0, w1, w2, a0, a1, a2, g1, g2, k_k, k_a, v_first, v_res, S, *, tm=256):
    T, D = x.shape
    has_vres = v_res is not None
    row = lambda i: (i, 0)
    const = lambda i: (0, 0)

    def lora(wa, wb):
        n = -(-wa.shape[1] // LANES) * LANES
        return _pad_cols(wa, n).astype(BF16), _pad_rows(wb, n).astype(BF16)

    w1p, w2p = lora(w1, w2)
    a1p, a2p = lora(a1, a2)
    g1p, g2p = lora(g1, g2)
    args = [x, x, gn, mix, w_rkv[0].astype(BF16), w_rkv[1].astype(BF16), w_rkv[2].astype(BF16),
            w0[None, :], w1p, w2p, a0[None, :], a1p, a2p, g1p, g2p, k_k[None, :], k_a[None, :], _head_sum_matrix()]
    full = lambda a: pl.BlockSpec(a.shape, const)
    in_specs = [pl.BlockSpec((tm, D), row),
                pl.BlockSpec((8, D), lambda i: (jnp.maximum(i * (tm // 8) - 1, 0), 0))]
    in_specs += [full(a) for a in args[2:]]
    if has_vres:
        v0, v1, v2 = v_res
        v1p, v2p = lora(v1, v2)
        extra = [v_first, v0[None, :], v1p, v2p]
        args += extra
        in_specs += [pl.BlockSpec((tm, D), row)] + [full(a) for a in extra[1:]]
    outs = pl.pallas_call(
        functools.partial(_rw_pre_kernel, has_vres=has_vres, seq_tiles=S // tm),
        grid=(T // tm,),
        in_specs=in_specs,
        out_specs=[pl.BlockSpec((tm, D), row)] * 7,
        out_shape=[jax.ShapeDtypeStruct((T, D), F32)] * 7,
        compiler_params=_cparams(("parallel",)),
        name="rw_pre",
    )(*args)
    return outs


def _rw_scan_kernel(r_ref, lw_ref, k_ref, v_ref, kn_ref, b_ref, y_ref, s_ref):
    L = RW_CHUNK

    @pl.when(pl.program_id(1) == 0)
    def _():
        s_ref[...] = jnp.zeros_like(s_ref)

    lw = lw_ref[...]
    tr = lax.broadcasted_iota(I32, (L, L), 0)
    tc = lax.broadcasted_iota(I32, (L, L), 1)
    tri = jnp.where(tr >= tc, 1.0, 0.0).astype(BF16)
    hi, mid, lo = _split3(lw)
    lwc = _dot(tri, hi) + _dot(tri, mid) + _dot(tri, lo)
    wl = lwc[L - 1:L, :]
    w_inc = jnp.exp(lwc)
    w_exc = jnp.exp(lwc - lw)
    w_inv = jnp.exp(-lwc)
    w_end = jnp.exp(wl - lwc)
    w_all = jnp.exp(wl)
    kv = k_ref[...]
    bv = b_ref[...]
    vv = v_ref[...]
    rt = r_ref[...] * w_inc
    at = -kn_ref[...] * w_exc
    kt = kv * w_inv
    bt = bv * w_inv
    ke = kv * w_end
    be = bv * w_end

    lane = lax.broadcasted_iota(I32, (1, LANES), 1)
    first = lane < HEAD_DIM

    def sm(x):
        return jnp.concatenate([jnp.where(first, x, 0.0), jnp.where(first, 0.0, x)], axis=0)

    n2 = 2 * L
    ri = lax.broadcasted_iota(I32, (2 * n2, 2 * n2), 0)
    ci = lax.broadcasted_iota(I32, (2 * n2, 2 * n2), 1)
    causal = (ri % L) + ri // n2 > (ci % L)
    bi = lax.broadcasted_iota(I32, (n2, n2), 0) // L
    bj = lax.broadcasted_iota(I32, (n2, n2), 1) // L
    blockdiag = bi == bj
    eye = jnp.where(lax.broadcasted_iota(I32, (n2, n2), 0) == lax.broadcasted_iota(I32, (n2, n2), 1), 1.0, 0.0)

    for p in range(N_PAIRS):
        sl = slice(p * LANES, (p + 1) * LANES)
        vs = sm(vv[:, sl])
        a_l = jnp.concatenate([sm(at[:, sl]), sm(rt[:, sl])], axis=0).astype(BF16)
        a_r = jnp.concatenate([sm(bt[:, sl]), sm(kt[:, sl])], axis=0).astype(BF16)
        m = jnp.where(causal, _dot_nt(a_l, a_r), 0.0)
        a_ab = m[:n2, :n2]
        pm = a_ab.astype(BF16)
        tm = eye + a_ab
        for _ in range(5):
            pm = _dot(pm, pm).astype(BF16)
            tm = tm + _dot(tm.astype(BF16), pm)
        s_old = s_ref[p]
        p12 = _dot_nt(a_l, s_old.astype(BF16))
        vsb = vs.astype(BF16)
        z = p12[:n2] + _dot(m[:n2, n2:].astype(BF16), vsb)
        u = _dot(tm.astype(BF16), z.astype(BF16))
        y = p12[n2:] + _dot(m[n2:, :].astype(BF16), jnp.concatenate([u.astype(BF16), vsb], axis=0))
        y_ref[:, sl] = y[:L] + y[L:]
        us = u[:L] + u[L:]
        upd = _dot_tn(jnp.concatenate([us, vv[:, sl]], axis=0).astype(BF16),
                      jnp.concatenate([be[:, sl], ke[:, sl]], axis=0).astype(BF16))
        s_ref[p] = s_old * w_all[:, sl] + jnp.where(blockdiag, upd, 0.0)


def _rw_scan(r, lw, k, v, kn, b, B, S):
    D = D_MODEL
    L = RW_CHUNK
    spec = pl.BlockSpec((None, L, D), lambda bb, c: (bb, c, 0))
    args = [a.reshape(B, S, D) for a in (r, lw, k, v, kn, b)]
    y = pl.pallas_call(
        _rw_scan_kernel,
        grid=(B, S // L),
        in_specs=[spec] * 6,
        out_specs=spec,
        out_shape=jax.ShapeDtypeStruct((B, S, D), F32),
        scratch_shapes=[pltpu.VMEM((N_PAIRS, LANES, LANES), F32)],
        compiler_params=_cparams(("parallel", "arbitrary")),
        name="rw_scan",
    )(*args)
    return y.reshape(B * S, D)


def _rw_post_kernel(y_ref, r_ref, k_ref, v_ref, g_ref, rk_ref, lnw_ref, lnb_ref, hp_ref, o_ref):
    hp = hp_ref[...]
    y = y_ref[...]
    inv = 1.0 / HEAD_DIM
    mu = _dot_split2(y, hp) * inv
    yc = y - mu
    var = _dot_split2(yc * yc, hp) * inv
    yn = yc * lax.rsqrt(var + RW_GN_EPS) * lnw_ref[...] + lnb_ref[...]
    bonus = _dot_split2(r_ref[...] * k_ref[...] * rk_ref[...], hp) * v_ref[...]
    o_ref[...] = ((yn + bonus) * g_ref[...]).astype(o_ref.dtype)


def _rw_post(y, r, k, v, g, r_k, ln_w, ln_b, *, tm=256):
    T, D = y.shape
    row = lambda i: (i, 0)
    const = lambda i: (0, 0)
    return pl.pallas_call(
        _rw_post_kernel,
        grid=(T // tm,),
        in_specs=[pl.BlockSpec((tm, D), row)] * 5 + [pl.BlockSpec((1, D), const)] * 3 + [pl.BlockSpec((D, D), const)],
        out_specs=pl.BlockSpec((tm, D), row),
        out_shape=jax.ShapeDtypeStruct((T, D), BF16),
        compiler_params=_cparams(("parallel",)),
        name="rw_post",
    )(y, r, k, v, g, r_k.reshape(1, D), ln_w[None, :], ln_b[None, :], _head_sum_matrix())


def _rwkv_mixer(x, gn, B, S, mix, w_rkv, w0, w1, w2, a0, a1, a2, g1, g2, k_k, k_a, r_k, ln_w, ln_b, v_first, v_res):
    r, lw, k, v, kn, b, g = _rw_pre(x, gn, mix, w_rkv, w0, w1, w2, a0, a1, a2, g1, g2, k_k, k_a, v_first, v_res, S)
    y = _rw_scan(r, lw, k, v, kn, b, B, S)
    return _rw_post(y, r, k, v, g, r_k, ln_w, ln_b), v


def _sb_mixer(x, gn, B, S, w_qkv):
    qkv = _norm_proj(x, gn, w_qkv.astype(BF16))
    return _sb_attention(qkv.reshape(B, S, 3 * D_MODEL), B, S).reshape(B * S, D_MODEL)


def _dsa_mixer(x, gn, B, S, w_in, q_norm, k_norm):
    qkv, qcat, kcat, wi = _dsa_proj(x, gn, w_in, q_norm, k_norm, B, S)
    o = _dsa_attention(qkv.reshape(B, S, 3 * D_MODEL), qcat, kcat, wi, B, S)
    return o.reshape(B * S, D_MODEL)


def kernel(x, norm_mix, norm_ffn, ffn_w_gu, ffn_w_down, rw_mix, rw_w_rkv, rw_w0, rw_w1, rw_w2, rw_a0, rw_a1, rw_a2, rw_g1, rw_g2, rw_v0, rw_v1, rw_v2, rw_k_k, rw_k_a, rw_r_k, rw_ln_w, rw_ln_b, rw_w_out, sb_w_qkv, sb_w_out, ds_w_in, ds_q_norm, ds_k_norm, ds_w_out):
    B, S, D = x.shape
    depth = norm_mix.shape[0]
    xf = x.reshape(B * S, D)
    v_first = None
    for i in range(depth):
        kind, j = i % 3, i // 3
        gn = norm_mix[i][None, :]
        if kind == 0:
            v_res = None if j == 0 else (rw_v0[j - 1], rw_v1[j - 1], rw_v2[j - 1])
            o, v_layer = _rwkv_mixer(xf, gn, B, S, rw_mix[j], rw_w_rkv[j], rw_w0[j], rw_w1[j], rw_w2[j],
                                     rw_a0[j], rw_a1[j], rw_a2[j], rw_g1[j], rw_g2[j], rw_k_k[j], rw_k_a[j],
                                     rw_r_k[j], rw_ln_w[j], rw_ln_b[j], v_first, v_res)
            if j == 0:
                v_first = v_layer
            w_out = rw_w_out[j]
        elif kind == 1:
            o = _sb_mixer(xf, gn, B, S, sb_w_qkv[j])
            w_out = sb_w_out[j]
        else:
            o = _dsa_mixer(xf, gn, B, S, ds_w_in[j], ds_q_norm[j], ds_k_norm[j])
            w_out = ds_w_out[j]
        xf = _ffn_layer(xf, o, w_out.astype(BF16), norm_ffn[i][None, :], ffn_w_gu[i].astype(BF16),
                        ffn_w_down[i].astype(BF16))
    return xf.reshape(B, S, D)
```

```python
import functools
import math

import jax
import jax.numpy as jnp
from jax import lax
from jax.experimental import pallas as pl
from jax.experimental.pallas import tpu as pltpu

F32 = jnp.float32
BF16 = jnp.bfloat16
I32 = jnp.int32

D_MODEL = 1024
HEAD_DIM = 64
N_HEADS = D_MODEL // HEAD_DIM
N_PAIRS = N_HEADS // 2
LANES = 128
RMS_EPS = 1e-6
ROPE_THETA = 10000.0
RW_GN_EPS = HEAD_DIM * 1e-5
RW_CHUNK = 64
DS_TOPK_MAX = 256
DS_IDX_HEADS = 8
DS_IDX_DIM = 64
DS_CHUNK = 64
NEG_BIG = -1e30
SB_DEAD_LOG = -105.0
VMEM_LIMIT = 56 * 1024 * 1024


def _cparams(sem):
    return pltpu.CompilerParams(dimension_semantics=sem, vmem_limit_bytes=VMEM_LIMIT)


def _dot(a, b):
    return jnp.dot(a, b, preferred_element_type=F32)


def _dot_nt(a, b):
    return lax.dot_general(a, b, (((1,), (1,)), ((), ())), preferred_element_type=F32)


def _dot_tn(a, b):
    return lax.dot_general(a, b, (((0,), (0,)), ((), ())), preferred_element_type=F32)


def _split2(x):
    hi = x.astype(BF16)
    lo = (x - hi.astype(F32)).astype(BF16)
    return hi, lo


def _split3(x):
    hi = x.astype(BF16)
    r1 = x - hi.astype(F32)
    mid = r1.astype(BF16)
    lo = (r1 - mid.astype(F32)).astype(BF16)
    return hi, mid, lo


HS_W = 256


def _head_sum(x, hp):
    hi, lo = _split2(x)
    outs = []
    for s in range(x.shape[1] // HS_W):
        sl = slice(s * HS_W, (s + 1) * HS_W)
        outs.append(_dot(jnp.concatenate([hi[:, sl], lo[:, sl]], axis=1), hp))
    return jnp.concatenate(outs, axis=1)


def _rms(x, g):
    ms = jnp.mean(x * x, axis=-1, keepdims=True)
    return x * lax.rsqrt(ms + RMS_EPS) * g


def _ffn_kernel(x_ref, o_ref, wo_ref, g_ref, wgu_ref, wd_ref, out_ref, x1_ref, hn_ref, acc_ref, act_ref, *, tf):
    F = wd_ref.shape[0]
    nf = F // tf
    x1 = x_ref[...] + _dot(o_ref[...], wo_ref[...])
    x1_ref[...] = x1
    hn_ref[...] = _rms(x1, g_ref[...]).astype(BF16)
    for f in range(nf + 1):
        if f < nf:
            hn = hn_ref[...]
            g = _dot(hn, wgu_ref[:, f * tf:(f + 1) * tf])
            u = _dot(hn, wgu_ref[:, F + f * tf:F + (f + 1) * tf])
        if f > 0:
            d = _dot(act_ref[(f - 1) % 2], wd_ref[(f - 1) * tf:f * tf, :])
            if f == 1:
                acc_ref[...] = d
            else:
                acc_ref[...] += d
        if f < nf:
            act_ref[f % 2] = (g * jax.nn.sigmoid(g) * u).astype(BF16)
    out_ref[...] = x1_ref[...] + acc_ref[...]


def _ffn_layer(x, o, w_out, g_ffn, w_gu, w_down, *, tm=512, tf=256):
    T, D = x.shape
    F = w_down.shape[0]
    assert T % tm == 0 and F % tf == 0
    row = lambda i: (i, 0)
    const = lambda i: (0, 0)
    resident = lambda a: pl.BlockSpec(a.shape, const, pipeline_mode=pl.Buffered(1))
    return pl.pallas_call(
        functools.partial(_ffn_kernel, tf=tf),
        grid=(T // tm,),
        in_specs=[
            pl.BlockSpec((tm, D), row),
            pl.BlockSpec((tm, D), row),
            resident(w_out),
            pl.BlockSpec((1, D), const),
            resident(w_gu),
            resident(w_down),
        ],
        out_specs=pl.BlockSpec((tm, D), row),
        out_shape=jax.ShapeDtypeStruct((T, D), F32),
        scratch_shapes=[pltpu.VMEM((tm, D), F32), pltpu.VMEM((tm, D), BF16), pltpu.VMEM((tm, D), F32),
                        pltpu.VMEM((2, tm, tf), BF16)],
        compiler_params=_cparams(("parallel",)),
        name="ffn",
    )(x, o, w_out, g_ffn, w_gu, w_down)


def _norm_proj_kernel(x_ref, g_ref, w_ref, out_ref, hn_ref):
    @pl.when(pl.program_id(1) == 0)
    def _():
        hn_ref[...] = _rms(x_ref[...], g_ref[...]).astype(BF16)

    out_ref[...] = _dot(hn_ref[...], w_ref[...]).astype(out_ref.dtype)


def _norm_proj(x, g, w, *, tm=512, tn=1024):
    T, D = x.shape
    N = w.shape[1]
    return pl.pallas_call(
        _norm_proj_kernel,
        grid=(T // tm, N // tn),
        in_specs=[
            pl.BlockSpec((tm, D), lambda i, j: (i, 0)),
            pl.BlockSpec((1, D), lambda i, j: (0, 0)),
            pl.BlockSpec((D, tn), lambda i, j: (0, j)),
        ],
        out_specs=pl.BlockSpec((tm, tn), lambda i, j: (i, j)),
        out_shape=jax.ShapeDtypeStruct((T, N), BF16),
        scratch_shapes=[pltpu.VMEM((tm, D), BF16)],
        compiler_params=_cparams(("parallel", "arbitrary")),
        name="norm_proj",
    )(x, g, w)


def _sb_kernel(q_ref, k_ref, v_ref, suo_ref, o_ref, c_ref, a_ref, zs_ref, hl_ref, *, tq, tk):
    qi = pl.program_id(2)
    lane = lax.broadcasted_iota(I32, (1, LANES), 1)
    first = lane < HEAD_DIM
    q = q_ref[...] * jnp.asarray(HEAD_DIM ** -0.5, BF16)
    zero = jnp.zeros_like(q)
    qh = (jnp.where(first, q, zero), jnp.where(first, zero, q))
    suo = suo_ref[...]
    c_ref[...] = jnp.zeros_like(c_ref)
    a_ref[...] = jnp.zeros_like(a_ref)
    qpos = qi * tq + lax.broadcasted_iota(I32, (tq, 1), 0)
    nsub = tq // tk

    def a_dots(j):
        kb = k_ref[pl.ds(pl.multiple_of(j * tq, tq), tq), :]
        return [_dot_nt(qh[h], kb) for h in range(2)]

    def a_rest(j, zz, slot, masked):
        if masked:
            earlier = (j * tq + lax.broadcasted_iota(I32, (1, tq), 1)) < qpos
        for h in range(2):
            z = zz[h]
            sp = jnp.maximum(z, 0.0) + jnp.log(1.0 + jnp.exp(-jnp.abs(z)))
            zs = z - sp
            if masked:
                sp = jnp.where(earlier, sp, 0.0)
                zs = jnp.where(earlier, zs, NEG_BIG)
            zs_ref[slot, h] = zs
            for u in range(nsub):
                hi, lo = _split2(sp[:, u * tk:(u + 1) * tk])
                hl_ref[slot, h, :, 2 * u * tk:(2 * u + 1) * tk] = hi
                hl_ref[slot, h, :, (2 * u + 1) * tk:(2 * u + 2) * tk] = lo

    def b_dots(slot):
        return [[_dot(hl_ref[slot, h, :, 2 * u * tk:(2 * u + 2) * tk], suo) for u in range(nsub)] for h in range(2)]

    def b_rest(j, slot, cs):
        vb = v_ref[pl.ds(pl.multiple_of(j * tq, tq), tq), :]
        zv = jnp.zeros_like(vb)
        vb2 = jnp.concatenate([jnp.where(first, vb, zv), jnp.where(first, zv, vb)], axis=0)
        att = []
        for h in range(2):
            c = c_ref[h]
            parts = [None] * nsub
            for u in reversed(range(nsub)):
                a = jnp.exp(zs_ref[slot, h, :, u * tk:(u + 1) * tk] + cs[h][u][:, :tk] + c)
                parts[u] = a.astype(BF16)
                c = c + cs[h][u][:, tk:]
            c_ref[h] = c
            att += parts
        a_ref[...] += _dot(jnp.concatenate(att, axis=1), vb2)

    def tile(j, masked):
        a_rest(j, a_dots(j), 0, masked)
        b_rest(j, 0, b_dots(0))

    def c_max():
        return jnp.max(jnp.maximum(c_ref[0], c_ref[1]))

    tile(qi, True)

    def keep_going(carry):
        t, cm = carry
        return jnp.logical_and(t < qi, cm > SB_DEAD_LOG)

    def step(carry):
        t, _ = carry
        tile(qi - 1 - t, False)
        return t + 1, c_max()

    lax.while_loop(keep_going, step, (jnp.int32(0), c_max()))

    o_ref[...] = a_ref[...].astype(o_ref.dtype)


def _sb_attention(qkv, B, S, *, tq=256, tk=128):
    D = D_MODEL
    r = lax.broadcasted_iota(I32, (tk, 2 * tk), 0)
    c = lax.broadcasted_iota(I32, (tk, 2 * tk), 1)
    suo = jnp.where((c >= tk) | (r > c), -1.0, 0.0).astype(BF16)
    suo = jnp.concatenate([suo, suo], axis=0)
    kern = functools.partial(_sb_kernel, tq=tq, tk=tk)
    return pl.pallas_call(
        kern,
        grid=(B, N_PAIRS, S // tq),
        in_specs=[
            pl.BlockSpec((None, tq, LANES), lambda b, p, i: (b, i, p)),
            pl.BlockSpec((None, S, LANES), lambda b, p, i: (b, 0, N_PAIRS + p)),
            pl.BlockSpec((None, S, LANES), lambda b, p, i: (b, 0, 2 * N_PAIRS + p)),
            pl.BlockSpec((2 * tk, 2 * tk), lambda b, p, i: (0, 0)),
        ],
        out_specs=pl.BlockSpec((None, tq, LANES), lambda b, p, i: (b, i, p)),
        out_shape=jax.ShapeDtypeStruct((B, S, D), BF16),
        scratch_shapes=[pltpu.VMEM((2, tq, tk), F32), pltpu.VMEM((tq, LANES), F32),
                        pltpu.VMEM((1, 2, tq, tq), F32), pltpu.VMEM((1, 2, tq, 2 * tq), BF16)],
        compiler_params=_cparams(("parallel", "parallel", "arbitrary")),
        name="sb_attn",
    )(qkv, qkv, qkv, suo)


def _rope_slabs(y, cos, sin):
    lane = lax.broadcasted_iota(I32, (1, LANES), 1)
    first = (lane % HEAD_DIM) < (HEAD_DIM // 2)
    outs = []
    for s in range(y.shape[1] // LANES):
        ys = y[:, s * LANES:(s + 1) * LANES]
        rot = jnp.where(first, pltpu.roll(ys, LANES - HEAD_DIM // 2, axis=1), pltpu.roll(ys, HEAD_DIM // 2, axis=1))
        outs.append(ys * cos[:, s * LANES:(s + 1) * LANES] + rot * sin[:, s * LANES:(s + 1) * LANES])
    return jnp.concatenate(outs, axis=1)


def _dsa_proj_kernel(x_ref, g_ref, w_ref, wih_ref, wil_ref, hp_ref, qn_ref, kn_ref, cos_ref, sin_ref,
                     qkv_ref, qcat_ref, kcat_ref, wi_ref, hh_ref, hl_ref, *, wi_scale):
    j = pl.program_id(1)
    nq = DS_IDX_HEADS * DS_IDX_DIM

    @pl.when(j == 0)
    def _():
        hn = _rms(x_ref[...], g_ref[...])
        hh, hl = _split2(hn)
        hh_ref[...] = hh
        hl_ref[...] = hl
        wih = wih_ref[...]
        idx = _dot(hh, wih) + _dot(hl, wih) + _dot(hh, wil_ref[...])
        cos = cos_ref[...]
        sin = sin_ref[...]
        lane = lax.broadcasted_iota(I32, (1, LANES), 1)
        first = lane < DS_IDX_DIM

        def hi_lo(x):
            hi = x.astype(BF16).astype(F32)
            return hi, x - hi

        qi = _rope_slabs(idx[:, :nq], cos[:, :nq], sin[:, :nq])
        qhi, qlo = hi_lo(qi)
        for h in range(DS_IDX_HEADS):
            sl = slice((h // 2) * LANES, (h // 2 + 1) * LANES)
            if h % 2 == 0:
                own_hi = qhi[:, sl]
                a = jnp.where(first, own_hi, pltpu.roll(qlo[:, sl], DS_IDX_DIM, axis=1))
            else:
                own_hi = pltpu.roll(qhi[:, sl], DS_IDX_DIM, axis=1)
                a = jnp.where(first, own_hi, qlo[:, sl])
            b = jnp.where(first, own_hi, 0.0)
            qcat_ref[:, 2 * h * LANES:(2 * h + 1) * LANES] = a.astype(BF16)
            qcat_ref[:, (2 * h + 1) * LANES:(2 * h + 2) * LANES] = b.astype(BF16)
        tail = idx[:, nq:nq + LANES]
        ki = jnp.where(first, _rope_slabs(tail, cos[:, :LANES], sin[:, :LANES]), 0.0)
        khi, klo = hi_lo(ki)
        kcat_ref[:, :LANES] = (khi + pltpu.roll(khi, DS_IDX_DIM, axis=1)).astype(BF16)
        kcat_ref[:, LANES:] = klo.astype(BF16)
        wi_ref[...] = tail * wi_scale

    y = _dot(hh_ref[...], w_ref[...])

    def normed(gn_ref, scale):
        ms = _head_sum(y * y, hp_ref[...]) * (1.0 / HEAD_DIM)
        yn = y * lax.rsqrt(ms + RMS_EPS) * gn_ref[...]
        out = _rope_slabs(yn, cos_ref[...], sin_ref[...])
        return (out * scale).astype(qkv_ref.dtype)

    @pl.when(j == 0)
    def _():
        qkv_ref[...] = normed(qn_ref, HEAD_DIM ** -0.5)

    @pl.when(j == 1)
    def _():
        qkv_ref[...] = normed(kn_ref, 1.0)

    @pl.when(j == 2)
    def _():
        qkv_ref[...] = y.astype(qkv_ref.dtype)


def _head_sum_matrix():
    r = (lax.broadcasted_iota(I32, (2 * HS_W, HS_W), 0) % HS_W) // HEAD_DIM
    c = lax.broadcasted_iota(I32, (2 * HS_W, HS_W), 1) // HEAD_DIM
    return jnp.where(r == c, 1.0, 0.0).astype(BF16)


def _rope_tables(S):
    half = HEAD_DIM // 2
    inv = 1.0 / (ROPE_THETA ** (jnp.arange(half, dtype=F32) / half))
    ang = jnp.arange(S, dtype=F32)[:, None] * inv[None, :]
    cos = jnp.concatenate([jnp.cos(ang), jnp.cos(ang)], axis=1)
    sin = jnp.concatenate([-jnp.sin(ang), jnp.sin(ang)], axis=1)
    return jnp.tile(cos, (1, N_HEADS)), jnp.tile(sin, (1, N_HEADS))


def _dsa_proj(x, g, w_in, q_norm, k_norm, B, S, *, tm=256):
    T, D = x.shape
    c3 = 3 * D
    nq = DS_IDX_HEADS * DS_IDX_DIM
    n_idx = nq + DS_IDX_DIM + DS_IDX_HEADS
    w_qkv = w_in[:, :c3].astype(BF16)
    w_idx = jnp.pad(w_in[:, c3:], ((0, 0), (0, nq + LANES - n_idx)))
    wih = w_idx.astype(BF16)
    wil = (w_idx - wih.astype(F32)).astype(BF16)
    cos, sin = _rope_tables(S)
    qn = jnp.tile(q_norm, N_HEADS)[None, :]
    kn = jnp.tile(k_norm, N_HEADS)[None, :]
    wi_scale = DS_IDX_HEADS ** -0.5 * DS_IDX_DIM ** -0.5
    nsb = S // tm
    row = lambda i, j: (i, 0)
    const = lambda i, j: (0, 0)
    pos = lambda i, j: (i % nsb, 0)
    return pl.pallas_call(
        functools.partial(_dsa_proj_kernel, wi_scale=wi_scale),
        grid=(T // tm, 3),
        in_specs=[
            pl.BlockSpec((tm, D), row),
            pl.BlockSpec((1, D), const),
            pl.BlockSpec((D, D), lambda i, j: (0, j)),
            pl.BlockSpec((D, nq + LANES), const),
            pl.BlockSpec((D, nq + LANES), const),
            pl.BlockSpec((2 * HS_W, HS_W), const),
            pl.BlockSpec((1, D), const),
            pl.BlockSpec((1, D), const),
            pl.BlockSpec((tm, D), pos),
            pl.BlockSpec((tm, D), pos),
        ],
        out_specs=[
            pl.BlockSpec((tm, D), lambda i, j: (i, j)),
            pl.BlockSpec((tm, 2 * LANES * DS_IDX_HEADS), row),
            pl.BlockSpec((tm, 2 * LANES), row),
            pl.BlockSpec((tm, LANES), row),
        ],
        out_shape=[
            jax.ShapeDtypeStruct((T, c3), BF16),
            jax.ShapeDtypeStruct((T, 2 * LANES * DS_IDX_HEADS), BF16),
            jax.ShapeDtypeStruct((T, 2 * LANES), BF16),
            jax.ShapeDtypeStruct((T, LANES), F32),
        ],
        scratch_shapes=[pltpu.VMEM((tm, D), BF16), pltpu.VMEM((tm, D), BF16)],
        compiler_params=_cparams(("parallel", "arbitrary")),
        name="dsa_proj",
    )(x, g, w_qkv, wih, wil, _head_sum_matrix(), qn, kn, cos, sin)


def _dsa_attn_kernel(q_ref, k_ref, v_ref, qcat_ref, kcat_ref, wi_ref, tri_ref, o_ref,
                     keys_ref, bias_ref, thr_ref, cand_ref, need_ref, eqb_ref, mx_ref, ls_ref, acc_ref,
                     *, tq, ta, rb, n_sel):
    i = pl.program_id(1)
    p = pl.program_id(2)
    lane = lax.broadcasted_iota(I32, (1, LANES), 1)
    row = lax.broadcasted_iota(I32, (tq, 1), 0)
    vis_row = i * tq + (row // DS_CHUNK + 1) * DS_CHUNK
    vis_tile = (i + 1) * tq
    n_a = (vis_tile + ta - 1) // ta

    @pl.when(p == 0)
    def _select():
        wi = wi_ref[...]

        def score_blk(c, _):
            cs = pl.multiple_of(c * ta, ta)
            kc = kcat_ref[pl.ds(cs, ta), :]
            sc = jnp.zeros((tq, ta), F32)
            for h in range(DS_IDX_HEADS):
                idx = _dot_nt(qcat_ref[:, 2 * h * LANES:(2 * h + 2) * LANES], kc)
                sc = sc + wi[:, DS_IDX_DIM + h:DS_IDX_DIM + h + 1] * jnp.maximum(idx, 0.0)
            sc = jnp.where(sc == 0.0, 0.0, sc)
            kpos = cs + lax.broadcasted_iota(I32, (1, ta), 1)
            sc = jnp.where(kpos < vis_row, sc, -jnp.inf)
            bits = lax.bitcast_convert_type(sc, I32)
            keys_ref[:, pl.ds(cs, ta)] = jnp.where(bits < 0, bits ^ jnp.int32(0x7FFFFFFF), bits)
            return 0

        lax.fori_loop(0, n_a, score_blk, 0)

        nrep = ta // LANES
        groups = [slice(g * rb, (g + 1) * rb) for g in range(tq // rb)]

        def count(cmps, strict):
            outs = []
            for g0 in range(0, len(groups), 2):
                def blk(c, parts, g0=g0):
                    cs = pl.multiple_of(c * ta, ta)
                    parts = list(parts)
                    for s in range(nrep):
                        for d in range(2):
                            kk = keys_ref[groups[g0 + d], pl.ds(cs + s * LANES, LANES)]
                            hit = (kk > cmps[g0 + d]) if strict else (kk >= cmps[g0 + d])
                            parts[d] = parts[d] + jnp.where(hit, 1, 0).astype(I32)
                    return tuple(parts)
                zero = jnp.zeros((rb, LANES), I32)
                outs += list(lax.fori_loop(0, n_a, blk, (zero, zero)))
            return [jnp.sum(pt, axis=1, keepdims=True) for pt in outs]

        thr_ref[...] = jnp.full(thr_ref.shape, jnp.iinfo(jnp.int32).min, I32)

        def bit_step(t, _):
            bit = lax.shift_left(jnp.int32(1), 31 - t)
            cands = [thr_ref[g, :] + bit for g in groups]
            cnts = count(cands, False)
            for g, cand, cnt in zip(groups, cands, cnts):
                thr_ref[g, :] = jnp.where(cnt >= n_sel, cand, thr_ref[g, :])
            return 0

        lax.fori_loop(0, 32, bit_step, 0)
        for g, cnt in zip(groups, count([thr_ref[g, :] for g in groups], True)):
            need_ref[g, :] = jnp.broadcast_to((n_sel - cnt).astype(F32), (rb, LANES))
        tri = tri_ref[...]

        def bias_blk(c, _):
            cs = pl.multiple_of(c * ta, ta)
            kk = keys_ref[:, pl.ds(cs, ta)]
            thr = jnp.concatenate([thr_ref[...]] * nrep, axis=1)
            eqf = jnp.where(kk == thr, 1.0, 0.0)
            ranks = _dot(eqf.astype(BF16), tri)
            rank = jnp.concatenate([eqb_ref[...]] * nrep, axis=1) + ranks[:, :ta]
            need = jnp.concatenate([need_ref[...]] * nrep, axis=1)
            sel = jnp.where(kk > thr, 1.0, jnp.where(rank < need, eqf, 0.0))
            kpos = cs + lax.broadcasted_iota(I32, (1, ta), 1)
            sel = jnp.where(kpos < vis_row, sel, 0.0)
            bias_ref[:, pl.ds(cs, ta)] = (sel - 1.0) * (-NEG_BIG)
            eqb_ref[...] += ranks[:, ta:]
            return 0

        eqb_ref[...] = jnp.zeros_like(eqb_ref)
        lax.fori_loop(0, n_a, bias_blk, 0)

        @pl.when(n_a % 2 == 1)
        def _():
            bias_ref[:, pl.ds(pl.multiple_of(n_a * ta, ta), ta)] = jnp.full((tq, ta), NEG_BIG, F32)

    q = q_ref[...]
    zero = jnp.zeros_like(q)
    first = lane < HEAD_DIM
    qh2 = (jnp.where(first, q, zero), jnp.where(first, zero, q))
    mx_ref[...] = jnp.full(mx_ref.shape, NEG_BIG, F32)

    n_pair = (n_a + 1) // 2

    def max_blk(c2, _):
        for u in range(2):
            cs = pl.multiple_of((2 * c2 + u) * ta, ta)
            kb = k_ref[pl.ds(cs, ta), :]
            bias = bias_ref[:, pl.ds(cs, ta)]
            for h in range(2):
                mx_ref[u, h] = jnp.maximum(mx_ref[u, h], _dot_nt(qh2[h], kb) + bias)
        return 0

    lax.fori_loop(0, n_pair, max_blk, 0)
    m = [jnp.max(jnp.maximum(mx_ref[0, h], mx_ref[1, h]), axis=1, keepdims=True) for h in range(2)]
    ls_ref[...] = jnp.zeros_like(ls_ref)
    acc_ref[...] = jnp.zeros_like(acc_ref)

    def attn_blk(c2, _):
        css = [pl.multiple_of((2 * c2 + u) * ta, ta) for u in range(2)]
        logits = [[_dot_nt(qh2[h], k_ref[pl.ds(cs, ta), :]) for h in range(2)] for cs in css]
        for u, cs in enumerate(css):
            vb = v_ref[pl.ds(cs, ta), :]
            zv = jnp.zeros_like(vb)
            vb2 = jnp.concatenate([jnp.where(first, vb, zv), jnp.where(first, zv, vb)], axis=0)
            bias = bias_ref[:, pl.ds(cs, ta)]
            ps = []
            for h in range(2):
                pr = jnp.exp(logits[u][h] + bias - m[h])
                ls_ref[u, h] += pr
                ps.append(pr.astype(BF16))
            acc_ref[u] += _dot(jnp.concatenate(ps, axis=1), vb2)
        return 0

    lax.fori_loop(0, n_pair, attn_blk, 0)
    den = jnp.where(first, jnp.sum(ls_ref[0, 0] + ls_ref[1, 0], axis=1, keepdims=True),
                    jnp.sum(ls_ref[0, 1] + ls_ref[1, 1], axis=1, keepdims=True))
    o_ref[...] = ((acc_ref[0] + acc_ref[1]) / den).astype(o_ref.dtype)


def _dsa_attention(qkv, qcat, kcat, wi, B, S, *, tq=256, ta=512, rb=64):
    D = D_MODEL
    n_sel = min(DS_TOPK_MAX, S // 4)
    assert S % (2 * ta) == 0 and S // LANES <= 256
    nqc = 2 * LANES * DS_IDX_HEADS
    r = lax.broadcasted_iota(I32, (ta, ta + LANES), 0)
    c = lax.broadcasted_iota(I32, (ta, ta + LANES), 1)
    tri = jnp.where((r < c) | (c >= ta), 1.0, 0.0).astype(BF16)
    kern = functools.partial(_dsa_attn_kernel, tq=tq, ta=ta, rb=rb, n_sel=n_sel)
    s_pad = S
    return pl.pallas_call(
        kern,
        grid=(B, S // tq, N_PAIRS),
        in_specs=[
            pl.BlockSpec((None, tq, LANES), lambda b, i, p: (b, i, p)),
            pl.BlockSpec((None, S, LANES), lambda b, i, p: (b, 0, N_PAIRS + p)),
            pl.BlockSpec((None, S, LANES), lambda b, i, p: (b, 0, 2 * N_PAIRS + p)),
            pl.BlockSpec((None, tq, nqc), lambda b, i, p: (b, i, 0)),
            pl.BlockSpec((None, S, 2 * LANES), lambda b, i, p: (b, 0, 0)),
            pl.BlockSpec((None, tq, LANES), lambda b, i, p: (b, i, 0)),
            pl.BlockSpec((ta, ta + LANES), lambda b, i, p: (0, 0)),
        ],
        out_specs=pl.BlockSpec((None, tq, LANES), lambda b, i, p: (b, i, p)),
        out_shape=jax.ShapeDtypeStruct((B, S, D), BF16),
        scratch_shapes=[
            pltpu.VMEM((tq, s_pad), I32),
            pltpu.VMEM((tq, s_pad), F32),
            pltpu.VMEM((tq, LANES), I32),
            pltpu.VMEM((tq, LANES), I32),
            pltpu.VMEM((tq, LANES), F32),
            pltpu.VMEM((tq, LANES), F32),
            pltpu.VMEM((2, 2, tq, ta), F32),
            pltpu.VMEM((2, 2, tq, ta), F32),
            pltpu.VMEM((2, tq, LANES), F32),
        ],
        compiler_params=_cparams(("parallel", "arbitrary", "arbitrary")),
        name="dsa_attn",
    )(qkv, qkv, qkv, qcat.reshape(B, S, nqc), kcat.reshape(B, S, 2 * LANES), wi.reshape(B, S, LANES), tri)


def _rw_pre_kernel(*refs, has_vres, seq_tiles):
    if has_vres:
        (x_ref, xp_ref, gn_ref, mix_ref, wr_ref, wk_ref, wv_ref, w0_ref, w1_ref, w2_ref, a0_ref, a1_ref, a2_ref,
         g1_ref, g2_ref, kk_ref, ka_ref, hp_ref, vf_ref, v0_ref, v1_ref, v2_ref,
         r_out, lw_out, k_out, v_out, kn_out, b_out, g_out) = refs
    else:
        (x_ref, xp_ref, gn_ref, mix_ref, wr_ref, wk_ref, wv_ref, w0_ref, w1_ref, w2_ref, a0_ref, a1_ref, a2_ref,
         g1_ref, g2_ref, kk_ref, ka_ref, hp_ref,
         r_out, lw_out, k_out, v_out, kn_out, b_out, g_out) = refs
    i = pl.program_id(0)
    gn = gn_ref[...]
    h = _rms(x_ref[...], gn)
    tm = h.shape[0]
    hprev = _rms(xp_ref[...], gn)[7:8, :]
    hprev = jnp.where(i % seq_tiles == 0, 0.0, hprev)
    rowi = lax.broadcasted_iota(I32, (tm, 1), 0)
    sh = jnp.where(rowi == 0, hprev, pltpu.roll(h, 1, axis=0))
    dlt = sh - h
    mix = mix_ref[...]

    def stream(j):
        return (h + dlt * mix[j:j + 1, :]).astype(BF16)

    r = _dot(stream(0), wr_ref[...])
    k = _dot(stream(1), wk_ref[...])
    xv = stream(2)
    v = _dot(xv, wv_ref[...])
    w = w0_ref[...] + _dot(jnp.tanh(_dot(stream(3), w1_ref[...])).astype(BF16), w2_ref[...])
    a = jax.nn.sigmoid(a0_ref[...] + _dot(_dot(stream(4), a1_ref[...]).astype(BF16), a2_ref[...]))
    g = _dot(jax.nn.sigmoid(_dot(stream(5), g1_ref[...])).astype(BF16), g2_ref[...])
    if has_vres:
        gate = jax.nn.sigmoid(v0_ref[...] + _dot(_dot(xv, v1_ref[...]).astype(BF16), v2_ref[...]))
        v = v + (vf_ref[...] - v) * gate
    logw = -(jnp.maximum(-w, 0.0) + jnp.log1p(jnp.exp(-jnp.abs(w)))) - 0.5
    lw_out[...] = -jnp.exp(logw)
    kk = k * kk_ref[...]
    ss = _head_sum(kk * kk, hp_ref[...])
    kk = kk * lax.rsqrt(jnp.maximum(ss, 1e-24))
    r_out[...] = r
    k_out[...] = k * (1.0 + (a - 1.0) * ka_ref[...])
    v_out[...] = v
    kn_out[...] = kk
    b_out[...] = kk * a
    g_out[...] = g


def _pad_cols(w, n):
    return jnp.pad(w, ((0, 0), (0, n - w.shape[1])))


def _pad_rows(w, n):
    return jnp.pad(w, ((0, n - w.shape[0]), (0, 0)))


def _rw_pre(x, gn, mix, w_rkv, w0, w1, w2, a0, a1, a2, g1, g2, k_k, k_a, v_first, v_res, S, *, tm=256):
    T, D = x.shape
    has_vres = v_res is not None
    row = lambda i: (i, 0)
    const = lambda i: (0, 0)

    def lora(wa, wb):
        n = -(-wa.shape[1] // LANES) * LANES
        return _pad_cols(wa, n).astype(BF16), _pad_rows(wb, n).astype(BF16)

    w1p, w2p = lora(w1, w2)
    a1p, a2p = lora(a1, a2)
    g1p, g2p = lora(g1, g2)
    args = [x, x, gn, mix, w_rkv[0].astype(BF16), w_rkv[1].astype(BF16), w_rkv[2].astype(BF16),
            w0[None, :], w1p, w2p, a0[None, :], a1p, a2p, g1p, g2p, k_k[None, :], k_a[None, :], _head_sum_matrix()]
    full = lambda a: pl.BlockSpec(a.shape, const)
    in_specs = [pl.BlockSpec((tm, D), row),
                pl.BlockSpec((8, D), lambda i: (jnp.maximum(i * (tm // 8) - 1, 0), 0))]
    in_specs += [full(a) for a in args[2:]]
    if has_vres:
        v0, v1, v2 = v_res
        v1p, v2p = lora(v1, v2)
        extra = [v_first, v0[None, :], v1p, v2p]
        args += extra
        in_specs += [pl.BlockSpec((tm, D), row)] + [full(a) for a in extra[1:]]
    outs = pl.pallas_call(
        functools.partial(_rw_pre_kernel, has_vres=has_vres, seq_tiles=S // tm),
        grid=(T // tm,),
        in_specs=in_specs,
        out_specs=[pl.BlockSpec((tm, D), row)] * 7,
        out_shape=[jax.ShapeDtypeStruct((T, D), F32)] * 7,
        compiler_params=_cparams(("parallel",)),
        name="rw_pre",
    )(*args)
    return outs


def _rw_scan_kernel(r_ref, lw_ref, k_ref, v_ref, kn_ref, b_ref, y_ref, s_ref):
    L = RW_CHUNK

    @pl.when(pl.program_id(1) == 0)
    def _():
        s_ref[...] = jnp.zeros_like(s_ref)

    lw = lw_ref[...]
    tr = lax.broadcasted_iota(I32, (L, L), 0)
    tc = lax.broadcasted_iota(I32, (L, L), 1)
    tri = jnp.where(tr >= tc, 1.0, 0.0).astype(BF16)
    hi, mid, lo = _split3(lw)
    lwc = _dot(tri, hi) + _dot(tri, mid) + _dot(tri, lo)
    wl = lwc[L - 1:L, :]
    w_inc = jnp.exp(lwc)
    w_exc = jnp.exp(lwc - lw)
    w_inv = jnp.exp(-lwc)
    w_end = jnp.exp(wl - lwc)
    w_all = jnp.exp(wl)
    kv = k_ref[...]
    bv = b_ref[...]
    vv = v_ref[...]
    rt = r_ref[...] * w_inc
    at = -kn_ref[...] * w_exc
    kt = kv * w_inv
    bt = bv * w_inv
    ke = kv * w_end
    be = bv * w_end

    lane = lax.broadcasted_iota(I32, (1, LANES), 1)
    first = lane < HEAD_DIM

    def sm(x):
        return jnp.concatenate([jnp.where(first, x, 0.0), jnp.where(first, 0.0, x)], axis=0)

    n2 = 2 * L
    ri = lax.broadcasted_iota(I32, (2 * n2, 2 * n2), 0)
    ci = lax.broadcasted_iota(I32, (2 * n2, 2 * n2), 1)
    causal = (ri % L) + ri // n2 > (ci % L)
    bi = lax.broadcasted_iota(I32, (n2, n2), 0) // L
    bj = lax.broadcasted_iota(I32, (n2, n2), 1) // L
    blockdiag = bi == bj
    eye = jnp.where(lax.broadcasted_iota(I32, (n2, n2), 0) == lax.broadcasted_iota(I32, (n2, n2), 1), 1.0, 0.0)

    prs = range(N_PAIRS)
    sls = [slice(p * LANES, (p + 1) * LANES) for p in prs]
    vsb = [sm(vv[:, sl]).astype(BF16) for sl in sls]
    a_l = [jnp.concatenate([sm(at[:, sl]), sm(rt[:, sl])], axis=0).astype(BF16) for sl in sls]
    a_r = [jnp.concatenate([sm(bt[:, sl]), sm(kt[:, sl])], axis=0).astype(BF16) for sl in sls]
    m = [jnp.where(causal, _dot_nt(a_l[p], a_r[p]), 0.0) for p in prs]
    s_old = [s_ref[p] for p in prs]
    p12 = [_dot_nt(a_l[p], s_old[p].astype(BF16)) for p in prs]
    pm = [m[p][:n2, :n2].astype(BF16) for p in prs]
    tm = [eye + m[p][:n2, :n2] for p in prs]
    for _ in range(5):
        pm = [_dot(pm[p], pm[p]).astype(BF16) for p in prs]
        tm = [tm[p] + _dot(tm[p].astype(BF16), pm[p]) for p in prs]
    z = [p12[p][:n2] + _dot(m[p][:n2, n2:].astype(BF16), vsb[p]) for p in prs]
    u = [_dot(tm[p].astype(BF16), z[p].astype(BF16)) for p in prs]
    y = [p12[p][n2:] + _dot(m[p][n2:, :].astype(BF16), jnp.concatenate([u[p].astype(BF16), vsb[p]], axis=0))
         for p in prs]
    upd = [_dot_tn(jnp.concatenate([u[p][:L] + u[p][L:], vv[:, sls[p]]], axis=0).astype(BF16),
                   jnp.concatenate([be[:, sls[p]], ke[:, sls[p]]], axis=0).astype(BF16)) for p in prs]
    for p in prs:
        y_ref[:, sls[p]] = y[p][:L] + y[p][L:]
        s_ref[p] = s_old[p] * w_all[:, sls[p]] + jnp.where(blockdiag, upd[p], 0.0)


def _rw_scan(r, lw, k, v, kn, b, B, S):
    D = D_MODEL
    L = RW_CHUNK
    spec = pl.BlockSpec((None, L, D), lambda bb, c: (bb, c, 0))
    args = [a.reshape(B, S, D) for a in (r, lw, k, v, kn, b)]
    y = pl.pallas_call(
        _rw_scan_kernel,
        grid=(B, S // L),
        in_specs=[spec] * 6,
        out_specs=spec,
        out_shape=jax.ShapeDtypeStruct((B, S, D), F32),
        scratch_shapes=[pltpu.VMEM((N_PAIRS, LANES, LANES), F32)],
        compiler_params=_cparams(("parallel", "arbitrary")),
        name="rw_scan",
    )(*args)
    return y.reshape(B * S, D)


def _rw_post_kernel(y_ref, r_ref, k_ref, v_ref, g_ref, rk_ref, lnw_ref, lnb_ref, hp_ref, o_ref):
    hp = hp_ref[...]
    y = y_ref[...]
    inv = 1.0 / HEAD_DIM
    mu = _head_sum(y, hp) * inv
    yc = y - mu
    var = _head_sum(yc * yc, hp) * inv
    yn = yc * lax.rsqrt(var + RW_GN_EPS) * lnw_ref[...] + lnb_ref[...]
    bonus = _head_sum(r_ref[...] * k_ref[...] * rk_ref[...], hp) * v_ref[...]
    o_ref[...] = ((yn + bonus) * g_ref[...]).astype(o_ref.dtype)


def _rw_post(y, r, k, v, g, r_k, ln_w, ln_b, *, tm=256):
    T, D = y.shape
    row = lambda i: (i, 0)
    const = lambda i: (0, 0)
    return pl.pallas_call(
        _rw_post_kernel,
        grid=(T // tm,),
        in_specs=[pl.BlockSpec((tm, D), row)] * 5 + [pl.BlockSpec((1, D), const)] * 3
        + [pl.BlockSpec((2 * HS_W, HS_W), const)],
        out_specs=pl.BlockSpec((tm, D), row),
        out_shape=jax.ShapeDtypeStruct((T, D), BF16),
        compiler_params=_cparams(("parallel",)),
        name="rw_post",
    )(y, r, k, v, g, r_k.reshape(1, D), ln_w[None, :], ln_b[None, :], _head_sum_matrix())


def _rwkv_mixer(x, gn, B, S, mix, w_rkv, w0, w1, w2, a0, a1, a2, g1, g2, k_k, k_a, r_k, ln_w, ln_b, v_first, v_res):
    r, lw, k, v, kn, b, g = _rw_pre(x, gn, mix, w_rkv, w0, w1, w2, a0, a1, a2, g1, g2, k_k, k_a, v_first, v_res, S)
    y = _rw_scan(r, lw, k, v, kn, b, B, S)
    return _rw_post(y, r, k, v, g, r_k, ln_w, ln_b), v


def _sb_mixer(x, gn, B, S, w_qkv):
    qkv = _norm_proj(x, gn, w_qkv.astype(BF16))
    return _sb_attention(qkv.reshape(B, S, 3 * D_MODEL), B, S).reshape(B * S, D_MODEL)


def _dsa_mixer(x, gn, B, S, w_in, q_norm, k_norm):
    qkv, qcat, kcat, wi = _dsa_proj(x, gn, w_in, q_norm, k_norm, B, S)
    o = _dsa_attention(qkv.reshape(B, S, 3 * D_MODEL), qcat, kcat, wi, B, S)
    return o.reshape(B * S, D_MODEL)


def kernel(x, norm_mix, norm_ffn, ffn_w_gu, ffn_w_down, rw_mix, rw_w_rkv, rw_w0, rw_w1, rw_w2, rw_a0, rw_a1, rw_a2, rw_g1, rw_g2, rw_v0, rw_v1, rw_v2, rw_k_k, rw_k_a, rw_r_k, rw_ln_w, rw_ln_b, rw_w_out, sb_w_qkv, sb_w_out, ds_w_in, ds_q_norm, ds_k_norm, ds_w_out):
    B, S, D = x.shape
    depth = norm_mix.shape[0]
    xf = x.reshape(B * S, D)
    v_first = None
    for i in range(depth):
        kind, j = i % 3, i // 3
        gn = norm_mix[i][None, :]
        if kind == 0:
            v_res = None if j == 0 else (rw_v0[j - 1], rw_v1[j - 1], rw_v2[j - 1])
            o, v_layer = _rwkv_mixer(xf, gn, B, S, rw_mix[j], rw_w_rkv[j], rw_w0[j], rw_w1[j], rw_w2[j],
                                     rw_a0[j], rw_a1[j], rw_a2[j], rw_g1[j], rw_g2[j], rw_k_k[j], rw_k_a[j],
                                     rw_r_k[j], rw_ln_w[j], rw_ln_b[j], v_first, v_res)
            if j == 0:
                v_first = v_layer
            w_out = rw_w_out[j]
        elif kind == 1:
            o = _sb_mixer(xf, gn, B, S, sb_w_qkv[j])
            w_out = sb_w_out[j]
        else:
            o = _dsa_mixer(xf, gn, B, S, ds_w_in[j], ds_q_norm[j], ds_k_norm[j])
            w_out = ds_w_out[j]
        xf = _ffn_layer(xf, o, w_out.astype(BF16), norm_ffn[i][None, :], ffn_w_gu[i].astype(BF16),
                        ffn_w_down[i].astype(BF16))
    return xf.reshape(B, S, D)
```

```python
import functools
import math

import jax
import jax.numpy as jnp
from jax import lax
from jax.experimental import pallas as pl
from jax.experimental.pallas import tpu as pltpu

F32 = jnp.float32
BF16 = jnp.bfloat16
I32 = jnp.int32

D_MODEL = 1024
HEAD_DIM = 64
N_HEADS = D_MODEL // HEAD_DIM
N_PAIRS = N_HEADS // 2
LANES = 128
RMS_EPS = 1e-6
ROPE_THETA = 10000.0
RW_GN_EPS = HEAD_DIM * 1e-5
RW_CHUNK = 64
DS_TOPK_MAX = 256
DS_IDX_HEADS = 8
DS_IDX_DIM = 64
DS_CHUNK = 64
NEG_BIG = -1e30
SB_DEAD_LOG = -105.0
DS_SAFE_LOGIT = 35.0
VMEM_LIMIT = 56 * 1024 * 1024


def _cparams(sem):
    return pltpu.CompilerParams(dimension_semantics=sem, vmem_limit_bytes=VMEM_LIMIT)


def _dot(a, b):
    return jnp.dot(a, b, preferred_element_type=F32)


def _dot_nt(a, b):
    return lax.dot_general(a, b, (((1,), (1,)), ((), ())), preferred_element_type=F32)


def _dot_tn(a, b):
    return lax.dot_general(a, b, (((0,), (0,)), ((), ())), preferred_element_type=F32)


def _split2(x):
    hi = x.astype(BF16)
    lo = (x - hi.astype(F32)).astype(BF16)
    return hi, lo


def _split3(x):
    hi = x.astype(BF16)
    r1 = x - hi.astype(F32)
    mid = r1.astype(BF16)
    lo = (r1 - mid.astype(F32)).astype(BF16)
    return hi, mid, lo


HS_W = 256


def _head_sum(x, hp):
    hi, lo = _split2(x)
    outs = []
    for s in range(x.shape[1] // HS_W):
        sl = slice(s * HS_W, (s + 1) * HS_W)
        outs.append(_dot(jnp.concatenate([hi[:, sl], lo[:, sl]], axis=1), hp))
    return jnp.concatenate(outs, axis=1)


def _rms(x, g):
    ms = jnp.mean(x * x, axis=-1, keepdims=True)
    return x * lax.rsqrt(ms + RMS_EPS) * g


def _ffn_kernel(x_ref, o_ref, wo_ref, g_ref, wgu_ref, wd_ref, out_ref, x1_ref, hn_ref, acc_ref, act_ref, *, tf):
    F = wd_ref.shape[0]
    nf = F // tf
    x1 = x_ref[...] + _dot(o_ref[...], wo_ref[...])
    x1_ref[...] = x1
    hn_ref[...] = _rms(x1, g_ref[...]).astype(BF16)
    for f in range(nf + 1):
        if f < nf:
            hn = hn_ref[...]
            g = _dot(hn, wgu_ref[:, f * tf:(f + 1) * tf])
            u = _dot(hn, wgu_ref[:, F + f * tf:F + (f + 1) * tf])
        if f > 0:
            d = _dot(act_ref[(f - 1) % 2], wd_ref[(f - 1) * tf:f * tf, :])
            if f == 1:
                acc_ref[...] = d
            else:
                acc_ref[...] += d
        if f < nf:
            act_ref[f % 2] = (g * jax.nn.sigmoid(g) * u).astype(BF16)
    out_ref[...] = x1_ref[...] + acc_ref[...]


def _ffn_layer(x, o, w_out, g_ffn, w_gu, w_down, *, tm=512, tf=256):
    T, D = x.shape
    F = w_down.shape[0]
    assert T % tm == 0 and F % tf == 0
    row = lambda i: (i, 0)
    const = lambda i: (0, 0)
    resident = lambda a: pl.BlockSpec(a.shape, const, pipeline_mode=pl.Buffered(1))
    return pl.pallas_call(
        functools.partial(_ffn_kernel, tf=tf),
        grid=(T // tm,),
        in_specs=[
            pl.BlockSpec((tm, D), row),
            pl.BlockSpec((tm, D), row),
            resident(w_out),
            pl.BlockSpec((1, D), const),
            resident(w_gu),
            resident(w_down),
        ],
        out_specs=pl.BlockSpec((tm, D), row),
        out_shape=jax.ShapeDtypeStruct((T, D), F32),
        scratch_shapes=[pltpu.VMEM((tm, D), F32), pltpu.VMEM((tm, D), BF16), pltpu.VMEM((tm, D), F32),
                        pltpu.VMEM((2, tm, tf), BF16)],
        compiler_params=_cparams(("parallel",)),
        name="ffn",
    )(x, o, w_out, g_ffn, w_gu, w_down)


def _norm_proj_kernel(x_ref, g_ref, w_ref, out_ref, hn_ref):
    @pl.when(pl.program_id(1) == 0)
    def _():
        hn_ref[...] = _rms(x_ref[...], g_ref[...]).astype(BF16)

    out_ref[...] = _dot(hn_ref[...], w_ref[...]).astype(out_ref.dtype)


def _norm_proj(x, g, w, *, tm=512, tn=1024):
    T, D = x.shape
    N = w.shape[1]
    return pl.pallas_call(
        _norm_proj_kernel,
        grid=(T // tm, N // tn),
        in_specs=[
            pl.BlockSpec((tm, D), lambda i, j: (i, 0)),
            pl.BlockSpec((1, D), lambda i, j: (0, 0)),
            pl.BlockSpec((D, tn), lambda i, j: (0, j)),
        ],
        out_specs=pl.BlockSpec((tm, tn), lambda i, j: (i, j)),
        out_shape=jax.ShapeDtypeStruct((T, N), BF16),
        scratch_shapes=[pltpu.VMEM((tm, D), BF16)],
        compiler_params=_cparams(("parallel", "arbitrary")),
        name="norm_proj",
    )(x, g, w)


def _sb_kernel(q_ref, k_ref, v_ref, suo_ref, o_ref, c_ref, a_ref, zs_ref, hl_ref, *, tq, tk):
    qi = pl.program_id(2)
    lane = lax.broadcasted_iota(I32, (1, LANES), 1)
    first = lane < HEAD_DIM
    q = q_ref[...] * jnp.asarray(HEAD_DIM ** -0.5, BF16)
    zero = jnp.zeros_like(q)
    qh = (jnp.where(first, q, zero), jnp.where(first, zero, q))
    suo = suo_ref[...]
    c_ref[...] = jnp.zeros_like(c_ref)
    a_ref[...] = jnp.zeros_like(a_ref)
    qpos = qi * tq + lax.broadcasted_iota(I32, (tq, 1), 0)
    nsub = tq // tk

    def a_dots(j):
        kb = k_ref[pl.ds(pl.multiple_of(j * tq, tq), tq), :]
        return [_dot_nt(qh[h], kb) for h in range(2)]

    def a_rest(j, zz, slot, masked):
        if masked:
            earlier = (j * tq + lax.broadcasted_iota(I32, (1, tq), 1)) < qpos
        for h in range(2):
            z = zz[h]
            sp = jnp.maximum(z, 0.0) + jnp.log(1.0 + jnp.exp(-jnp.abs(z)))
            zs = z - sp
            if masked:
                sp = jnp.where(earlier, sp, 0.0)
                zs = jnp.where(earlier, zs, NEG_BIG)
            zs_ref[slot, h] = zs
            for u in range(nsub):
                hi, lo = _split2(sp[:, u * tk:(u + 1) * tk])
                hl_ref[slot, h, :, 2 * u * tk:(2 * u + 1) * tk] = hi
                hl_ref[slot, h, :, (2 * u + 1) * tk:(2 * u + 2) * tk] = lo

    def b_dots(slot):
        return [[_dot(hl_ref[slot, h, :, 2 * u * tk:(2 * u + 2) * tk], suo) for u in range(nsub)] for h in range(2)]

    def b_rest(j, slot, cs):
        vb = v_ref[pl.ds(pl.multiple_of(j * tq, tq), tq), :]
        zv = jnp.zeros_like(vb)
        vb2 = jnp.concatenate([jnp.where(first, vb, zv), jnp.where(first, zv, vb)], axis=0)
        att = []
        for h in range(2):
            c = c_ref[h]
            parts = [None] * nsub
            for u in reversed(range(nsub)):
                a = jnp.exp(zs_ref[slot, h, :, u * tk:(u + 1) * tk] + cs[h][u][:, :tk] + c)
                parts[u] = a.astype(BF16)
                c = c + cs[h][u][:, tk:]
            c_ref[h] = c
            att += parts
        a_ref[...] += _dot(jnp.concatenate(att, axis=1), vb2)

    def tile(j, masked):
        a_rest(j, a_dots(j), 0, masked)
        b_rest(j, 0, b_dots(0))

    def c_max():
        return jnp.max(jnp.maximum(c_ref[0], c_ref[1]))

    tile(qi, True)

    def keep_going(carry):
        t, cm = carry
        return jnp.logical_and(t < qi, cm > SB_DEAD_LOG)

    def step(carry):
        t, _ = carry
        tile(qi - 1 - t, False)
        return t + 1, c_max()

    lax.while_loop(keep_going, step, (jnp.int32(0), c_max()))

    o_ref[...] = a_ref[...].astype(o_ref.dtype)


def _sb_attention(qkv, B, S, *, tq=256, tk=128):
    D = D_MODEL
    r = lax.broadcasted_iota(I32, (tk, 2 * tk), 0)
    c = lax.broadcasted_iota(I32, (tk, 2 * tk), 1)
    suo = jnp.where((c >= tk) | (r > c), -1.0, 0.0).astype(BF16)
    suo = jnp.concatenate([suo, suo], axis=0)
    kern = functools.partial(_sb_kernel, tq=tq, tk=tk)
    return pl.pallas_call(
        kern,
        grid=(B, N_PAIRS, S // tq),
        in_specs=[
            pl.BlockSpec((None, tq, LANES), lambda b, p, i: (b, i, p)),
            pl.BlockSpec((None, S, LANES), lambda b, p, i: (b, 0, N_PAIRS + p)),
            pl.BlockSpec((None, S, LANES), lambda b, p, i: (b, 0, 2 * N_PAIRS + p)),
            pl.BlockSpec((2 * tk, 2 * tk), lambda b, p, i: (0, 0)),
        ],
        out_specs=pl.BlockSpec((None, tq, LANES), lambda b, p, i: (b, i, p)),
        out_shape=jax.ShapeDtypeStruct((B, S, D), BF16),
        scratch_shapes=[pltpu.VMEM((2, tq, tk), F32), pltpu.VMEM((tq, LANES), F32),
                        pltpu.VMEM((1, 2, tq, tq), F32), pltpu.VMEM((1, 2, tq, 2 * tq), BF16)],
        compiler_params=_cparams(("parallel", "parallel", "arbitrary")),
        name="sb_attn",
    )(qkv, qkv, qkv, suo)


def _rope_slabs(y, cos, sin):
    lane = lax.broadcasted_iota(I32, (1, LANES), 1)
    first = (lane % HEAD_DIM) < (HEAD_DIM // 2)
    outs = []
    for s in range(y.shape[1] // LANES):
        ys = y[:, s * LANES:(s + 1) * LANES]
        rot = jnp.where(first, pltpu.roll(ys, LANES - HEAD_DIM // 2, axis=1), pltpu.roll(ys, HEAD_DIM // 2, axis=1))
        outs.append(ys * cos[:, s * LANES:(s + 1) * LANES] + rot * sin[:, s * LANES:(s + 1) * LANES])
    return jnp.concatenate(outs, axis=1)


def _dsa_proj_kernel(x_ref, g_ref, w_ref, wih_ref, wil_ref, hp_ref, qn_ref, kn_ref, cos_ref, sin_ref,
                     qkv_ref, qcat_ref, kcat_ref, wi_ref, kmx_ref, hh_ref, hl_ref, *, wi_scale):
    j = pl.program_id(1)
    nq = DS_IDX_HEADS * DS_IDX_DIM

    @pl.when(j == 0)
    def _():
        hn = _rms(x_ref[...], g_ref[...])
        hh, hl = _split2(hn)
        hh_ref[...] = hh
        hl_ref[...] = hl
        wih = wih_ref[...]
        idx = _dot(hh, wih) + _dot(hl, wih) + _dot(hh, wil_ref[...])
        cos = cos_ref[...]
        sin = sin_ref[...]
        lane = lax.broadcasted_iota(I32, (1, LANES), 1)
        first = lane < DS_IDX_DIM

        def hi_lo(x):
            hi = x.astype(BF16).astype(F32)
            return hi, x - hi

        qi = _rope_slabs(idx[:, :nq], cos[:, :nq], sin[:, :nq])
        qhi, qlo = hi_lo(qi)
        for h in range(DS_IDX_HEADS):
            sl = slice((h // 2) * LANES, (h // 2 + 1) * LANES)
            if h % 2 == 0:
                own_hi = qhi[:, sl]
                a = jnp.where(first, own_hi, pltpu.roll(qlo[:, sl], DS_IDX_DIM, axis=1))
            else:
                own_hi = pltpu.roll(qhi[:, sl], DS_IDX_DIM, axis=1)
                a = jnp.where(first, own_hi, qlo[:, sl])
            b = jnp.where(first, own_hi, 0.0)
            qcat_ref[:, 2 * h * LANES:(2 * h + 1) * LANES] = a.astype(BF16)
            qcat_ref[:, (2 * h + 1) * LANES:(2 * h + 2) * LANES] = b.astype(BF16)
        tail = idx[:, nq:nq + LANES]
        ki = jnp.where(first, _rope_slabs(tail, cos[:, :LANES], sin[:, :LANES]), 0.0)
        khi, klo = hi_lo(ki)
        kcat_ref[:, :LANES] = (khi + pltpu.roll(khi, DS_IDX_DIM, axis=1)).astype(BF16)
        kcat_ref[:, LANES:] = klo.astype(BF16)
        wi_ref[...] = tail * wi_scale

    y = _dot(hh_ref[...], w_ref[...])

    def normed(gn_ref, scale):
        ms = _head_sum(y * y, hp_ref[...]) * (1.0 / HEAD_DIM)
        yn = y * lax.rsqrt(ms + RMS_EPS) * gn_ref[...]
        out = _rope_slabs(yn, cos_ref[...], sin_ref[...])
        return (out * scale).astype(qkv_ref.dtype)

    @pl.when(j == 0)
    def _():
        qkv_ref[...] = normed(qn_ref, HEAD_DIM ** -0.5)

    @pl.when(j == 1)
    def _():
        kb = normed(kn_ref, 1.0)
        qkv_ref[...] = kb
        kf = kb.astype(F32)
        kmx_ref[...] = jnp.max(_head_sum(kf * kf, hp_ref[...]), axis=0, keepdims=True)

    @pl.when(j == 2)
    def _():
        qkv_ref[...] = y.astype(qkv_ref.dtype)


def _head_sum_matrix():
    r = (lax.broadcasted_iota(I32, (2 * HS_W, HS_W), 0) % HS_W) // HEAD_DIM
    c = lax.broadcasted_iota(I32, (2 * HS_W, HS_W), 1) // HEAD_DIM
    return jnp.where(r == c, 1.0, 0.0).astype(BF16)


def _rope_tables(S):
    half = HEAD_DIM // 2
    inv = 1.0 / (ROPE_THETA ** (jnp.arange(half, dtype=F32) / half))
    ang = jnp.arange(S, dtype=F32)[:, None] * inv[None, :]
    cos = jnp.concatenate([jnp.cos(ang), jnp.cos(ang)], axis=1)
    sin = jnp.concatenate([-jnp.sin(ang), jnp.sin(ang)], axis=1)
    return jnp.tile(cos, (1, N_HEADS)), jnp.tile(sin, (1, N_HEADS))


def _dsa_proj(x, g, w_in, q_norm, k_norm, B, S, *, tm=256):
    T, D = x.shape
    c3 = 3 * D
    nq = DS_IDX_HEADS * DS_IDX_DIM
    n_idx = nq + DS_IDX_DIM + DS_IDX_HEADS
    w_qkv = w_in[:, :c3].astype(BF16)
    w_idx = jnp.pad(w_in[:, c3:], ((0, 0), (0, nq + LANES - n_idx)))
    wih = w_idx.astype(BF16)
    wil = (w_idx - wih.astype(F32)).astype(BF16)
    cos, sin = _rope_tables(S)
    qn = jnp.tile(q_norm, N_HEADS)[None, :]
    kn = jnp.tile(k_norm, N_HEADS)[None, :]
    wi_scale = DS_IDX_HEADS ** -0.5 * DS_IDX_DIM ** -0.5
    nsb = S // tm
    row = lambda i, j: (i, 0)
    const = lambda i, j: (0, 0)
    pos = lambda i, j: (i % nsb, 0)
    return pl.pallas_call(
        functools.partial(_dsa_proj_kernel, wi_scale=wi_scale),
        grid=(T // tm, 3),
        in_specs=[
            pl.BlockSpec((tm, D), row),
            pl.BlockSpec((1, D), const),
            pl.BlockSpec((D, D), lambda i, j: (0, j)),
            pl.BlockSpec((D, nq + LANES), const),
            pl.BlockSpec((D, nq + LANES), const),
            pl.BlockSpec((2 * HS_W, HS_W), const),
            pl.BlockSpec((1, D), const),
            pl.BlockSpec((1, D), const),
            pl.BlockSpec((tm, D), pos),
            pl.BlockSpec((tm, D), pos),
        ],
        out_specs=[
            pl.BlockSpec((tm, D), lambda i, j: (i, j)),
            pl.BlockSpec((tm, 2 * LANES * DS_IDX_HEADS), row),
            pl.BlockSpec((tm, 2 * LANES), row),
            pl.BlockSpec((tm, LANES), row),
            pl.BlockSpec((None, 1, D), lambda i, j: (i, 0, 0)),
        ],
        out_shape=[
            jax.ShapeDtypeStruct((T, c3), BF16),
            jax.ShapeDtypeStruct((T, 2 * LANES * DS_IDX_HEADS), BF16),
            jax.ShapeDtypeStruct((T, 2 * LANES), BF16),
            jax.ShapeDtypeStruct((T, LANES), F32),
            jax.ShapeDtypeStruct((T // tm, 1, D), F32),
        ],
        scratch_shapes=[pltpu.VMEM((tm, D), BF16), pltpu.VMEM((tm, D), BF16)],
        compiler_params=_cparams(("parallel", "arbitrary")),
        name="dsa_proj",
    )(x, g, w_qkv, wih, wil, _head_sum_matrix(), qn, kn, cos, sin)


def _dsa_attn_kernel(q_ref, k_ref, v_ref, qcat_ref, kcat_ref, wi_ref, kmx_ref, tri_ref, o_ref,
                     keys_ref, bias_ref, thr_ref, need_ref, eqb_ref, m_ref, mx_ref, acc_ref,
                     *, tq, ta, rb, n_sel):
    i = pl.program_id(1)
    p = pl.program_id(2)
    lane = lax.broadcasted_iota(I32, (1, LANES), 1)
    row = lax.broadcasted_iota(I32, (tq, 1), 0)
    vis_row = i * tq + (row // DS_CHUNK + 1) * DS_CHUNK
    vis_tile = (i + 1) * tq
    n_a = (vis_tile + ta - 1) // ta

    @pl.when(p == 0)
    def _select():
        wi = wi_ref[...]

        def score_blk(c, _):
            cs = pl.multiple_of(c * ta, ta)
            kc = kcat_ref[pl.ds(cs, ta), :]
            sc = jnp.zeros((tq, ta), F32)
            for h in range(DS_IDX_HEADS):
                idx = _dot_nt(qcat_ref[:, 2 * h * LANES:(2 * h + 2) * LANES], kc)
                sc = sc + wi[:, DS_IDX_DIM + h:DS_IDX_DIM + h + 1] * jnp.maximum(idx, 0.0)
            sc = jnp.where(sc == 0.0, 0.0, sc)
            kpos = cs + lax.broadcasted_iota(I32, (1, ta), 1)
            sc = jnp.where(kpos < vis_row, sc, -jnp.inf)
            bits = lax.bitcast_convert_type(sc, I32)
            keys_ref[:, pl.ds(cs, ta)] = jnp.where(bits < 0, bits ^ jnp.int32(0x7FFFFFFF), bits)
            return 0

        lax.fori_loop(0, n_a, score_blk, 0)
        int_min = jnp.iinfo(jnp.int32).min

        @pl.when(n_a % 2 == 1)
        def _():
            keys_ref[:, pl.ds(pl.multiple_of(n_a * ta, ta), ta)] = jnp.full((tq, ta), int_min, I32)

        nsweep = 2 * ta // LANES
        groups = [slice(g * rb, (g + 1) * rb) for g in range(tq // rb)]

        def count(cmps, strict):
            outs = []
            for g0 in range(0, len(groups), 2):
                def blk(c, parts, g0=g0):
                    cs = pl.multiple_of(c * 2 * ta, 2 * ta)
                    parts = list(parts)
                    for s in range(nsweep):
                        for d in range(2):
                            kk = keys_ref[groups[g0 + d], pl.ds(cs + s * LANES, LANES)]
                            hit = (kk > cmps[g0 + d]) if strict else (kk >= cmps[g0 + d])
                            parts[d] = parts[d] + jnp.where(hit, 1, 0).astype(I32)
                    return tuple(parts)
                zero = jnp.zeros((rb, LANES), I32)
                outs += list(lax.fori_loop(0, (n_a + 1) // 2, blk, (zero, zero)))
            return [jnp.sum(pt, axis=1, keepdims=True) for pt in outs]

        thr_ref[...] = jnp.full(thr_ref.shape, jnp.iinfo(jnp.int32).min, I32)

        def bit_step(t, _):
            bit = lax.shift_left(jnp.int32(1), 31 - t)
            cands = [thr_ref[g, :] + bit for g in groups]
            cnts = count(cands, False)
            for g, cand, cnt in zip(groups, cands, cnts):
                thr_ref[g, :] = jnp.where(cnt >= n_sel, cand, thr_ref[g, :])
            return 0

        lax.fori_loop(0, 32, bit_step, 0)
        for g, cnt in zip(groups, count([thr_ref[g, :] for g in groups], True)):
            need_ref[g, :] = jnp.broadcast_to((n_sel - cnt).astype(F32), (rb, LANES))
        tri = tri_ref[...]
        nrep = ta // LANES

        def bias_blk(c, _):
            cs = pl.multiple_of(c * ta, ta)
            kk = keys_ref[:, pl.ds(cs, ta)]
            thr = jnp.concatenate([thr_ref[...]] * nrep, axis=1)
            eqf = jnp.where(kk == thr, 1.0, 0.0)
            ranks = _dot(eqf.astype(BF16), tri)
            rank = jnp.concatenate([eqb_ref[...]] * nrep, axis=1) + ranks[:, :ta]
            need = jnp.concatenate([need_ref[...]] * nrep, axis=1)
            sel = jnp.where(kk > thr, 1.0, jnp.where(rank < need, eqf, 0.0))
            kpos = cs + lax.broadcasted_iota(I32, (1, ta), 1)
            sel = jnp.where(kpos < vis_row, sel, 0.0)
            bias_ref[:, pl.ds(cs, ta)] = (sel - 1.0) * (-NEG_BIG)
            eqb_ref[...] += ranks[:, ta:]
            return 0

        eqb_ref[...] = jnp.zeros_like(eqb_ref)
        lax.fori_loop(0, n_a, bias_blk, 0)

        @pl.when(n_a % 2 == 1)
        def _():
            bias_ref[:, pl.ds(pl.multiple_of(n_a * ta, ta), ta)] = jnp.full((tq, ta), NEG_BIG, F32)

    q = q_ref[...]
    zero = jnp.zeros_like(q)
    first = lane < HEAD_DIM
    qh2 = (jnp.where(first, q, zero), jnp.where(first, zero, q))
    n_pair = (n_a + 1) // 2

    trow = lax.broadcasted_iota(I32, (kmx_ref.shape[0], 1), 0)
    kmax2 = jnp.max(jnp.where(trow <= i, kmx_ref[...], 0.0), axis=0, keepdims=True)
    hr = lax.broadcasted_iota(I32, (LANES, LANES), 0) // HEAD_DIM
    hc = lax.broadcasted_iota(I32, (LANES, LANES), 1) // HEAD_DIM
    qf = q.astype(F32)
    q2 = _dot((qf * qf).astype(BF16), jnp.where(hr == hc, 1.0, 0.0).astype(BF16))
    bound2 = 1.05 * jnp.max(q2 * kmax2)
    m_ref[...] = jnp.zeros_like(m_ref)

    @pl.when(bound2 > DS_SAFE_LOGIT * DS_SAFE_LOGIT)
    def _row_max():
        mx_ref[...] = jnp.full(mx_ref.shape, NEG_BIG, F32)

        def max_blk(c2, _):
            for u in range(2):
                cs = pl.multiple_of((2 * c2 + u) * ta, ta)
                kb = k_ref[pl.ds(cs, ta), :]
                bias = bias_ref[:, pl.ds(cs, ta)]
                for h in range(2):
                    mx_ref[u, h] = jnp.maximum(mx_ref[u, h], _dot_nt(qh2[h], kb) + bias)
            return 0

        lax.fori_loop(0, n_pair, max_blk, 0)
        for h in range(2):
            m_ref[h] = jnp.broadcast_to(jnp.max(jnp.maximum(mx_ref[0, h], mx_ref[1, h]), axis=1, keepdims=True),
                                        (tq, LANES))

    ones_h = [jnp.broadcast_to(jnp.where(first, a, b), (ta, LANES)).astype(BF16) for a, b in ((1.0, 0.0), (0.0, 1.0))]
    acc_ref[...] = jnp.zeros_like(acc_ref)

    def attn_blk(c2, _):
        css = [pl.multiple_of((2 * c2 + u) * ta, ta) for u in range(2)]
        logits = [[_dot_nt(qh2[h], k_ref[pl.ds(cs, ta), :]) for h in range(2)] for cs in css]
        ps = []
        vs = []
        for u, cs in enumerate(css):
            vb = v_ref[pl.ds(cs, ta), :]
            zv = jnp.zeros_like(vb)
            vs += [jnp.concatenate([jnp.where(first, vb, zv), ones_h[0]], axis=1),
                   jnp.concatenate([jnp.where(first, zv, vb), ones_h[1]], axis=1)]
            bias = bias_ref[:, pl.ds(cs, ta)]
            for h in range(2):
                ps.append(jnp.exp(logits[u][h] + bias - m_ref[h][:, :1]).astype(BF16))
        acc_ref[...] += _dot(jnp.concatenate(ps, axis=1), jnp.concatenate(vs, axis=0))
        return 0

    lax.fori_loop(0, n_pair, attn_blk, 0)
    acc = acc_ref[...]
    o_ref[...] = (acc[:, :LANES] / acc[:, LANES:]).astype(o_ref.dtype)


def _dsa_attention(qkv, qcat, kcat, wi, kmx, B, S, *, tq=256, ta=512, rb=64):
    D = D_MODEL
    n_sel = min(DS_TOPK_MAX, S // 4)
    assert S % (2 * ta) == 0 and S // LANES <= 256 and kmx.shape[0] * tq == B * S
    nqc = 2 * LANES * DS_IDX_HEADS
    r = lax.broadcasted_iota(I32, (ta, ta + LANES), 0)
    c = lax.broadcasted_iota(I32, (ta, ta + LANES), 1)
    tri = jnp.where((r < c) | (c >= ta), 1.0, 0.0).astype(BF16)
    kern = functools.partial(_dsa_attn_kernel, tq=tq, ta=ta, rb=rb, n_sel=n_sel)
    s_pad = S
    return pl.pallas_call(
        kern,
        grid=(B, S // tq, N_PAIRS),
        in_specs=[
            pl.BlockSpec((None, tq, LANES), lambda b, i, p: (b, i, p)),
            pl.BlockSpec((None, S, LANES), lambda b, i, p: (b, 0, N_PAIRS + p)),
            pl.BlockSpec((None, S, LANES), lambda b, i, p: (b, 0, 2 * N_PAIRS + p)),
            pl.BlockSpec((None, tq, nqc), lambda b, i, p: (b, i, 0)),
            pl.BlockSpec((None, S, 2 * LANES), lambda b, i, p: (b, 0, 0)),
            pl.BlockSpec((None, tq, LANES), lambda b, i, p: (b, i, 0)),
            pl.BlockSpec((None, S // tq, LANES), lambda b, i, p: (b, 0, p)),
            pl.BlockSpec((ta, ta + LANES), lambda b, i, p: (0, 0)),
        ],
        out_specs=pl.BlockSpec((None, tq, LANES), lambda b, i, p: (b, i, p)),
        out_shape=jax.ShapeDtypeStruct((B, S, D), BF16),
        scratch_shapes=[
            pltpu.VMEM((tq, s_pad), I32),
            pltpu.VMEM((tq, s_pad), F32),
            pltpu.VMEM((tq, LANES), I32),
            pltpu.VMEM((tq, LANES), F32),
            pltpu.VMEM((tq, LANES), F32),
            pltpu.VMEM((2, tq, LANES), F32),
            pltpu.VMEM((2, 2, tq, ta), F32),
            pltpu.VMEM((tq, 2 * LANES), F32),
        ],
        compiler_params=_cparams(("parallel", "arbitrary", "arbitrary")),
        name="dsa_attn",
    )(qkv, qkv, qkv, qcat.reshape(B, S, nqc), kcat.reshape(B, S, 2 * LANES), wi.reshape(B, S, LANES),
      kmx.reshape(B, S // tq, D), tri)


def _rw_pre_kernel(*refs, has_vres, seq_tiles):
    if has_vres:
        (x_ref, xp_ref, gn_ref, mix_ref, wr_ref, wk_ref, wv_ref, w0_ref, w1_ref, w2_ref, a0_ref, a1_ref, a2_ref,
         g1_ref, g2_ref, kk_ref, ka_ref, hp_ref, vf_ref, v0_ref, v1_ref, v2_ref,
         r_out, lw_out, k_out, v_out, kn_out, b_out, g_out) = refs
    else:
        (x_ref, xp_ref, gn_ref, mix_ref, wr_ref, wk_ref, wv_ref, w0_ref, w1_ref, w2_ref, a0_ref, a1_ref, a2_ref,
         g1_ref, g2_ref, kk_ref, ka_ref, hp_ref,
         r_out, lw_out, k_out, v_out, kn_out, b_out, g_out) = refs
    i = pl.program_id(0)
    gn = gn_ref[...]
    h = _rms(x_ref[...], gn)
    tm = h.shape[0]
    hprev = _rms(xp_ref[...], gn)[7:8, :]
    hprev = jnp.where(i % seq_tiles == 0, 0.0, hprev)
    rowi = lax.broadcasted_iota(I32, (tm, 1), 0)
    sh = jnp.where(rowi == 0, hprev, pltpu.roll(h, 1, axis=0))
    dlt = sh - h
    mix = mix_ref[...]

    def stream(j):
        return (h + dlt * mix[j:j + 1, :]).astype(BF16)

    r = _dot(stream(0), wr_ref[...])
    k = _dot(stream(1), wk_ref[...])
    xv = stream(2)
    v = _dot(xv, wv_ref[...])
    w = w0_ref[...] + _dot(jnp.tanh(_dot(stream(3), w1_ref[...])).astype(BF16), w2_ref[...])
    a = jax.nn.sigmoid(a0_ref[...] + _dot(_dot(stream(4), a1_ref[...]).astype(BF16), a2_ref[...]))
    g = _dot(jax.nn.sigmoid(_dot(stream(5), g1_ref[...])).astype(BF16), g2_ref[...])
    if has_vres:
        gate = jax.nn.sigmoid(v0_ref[...] + _dot(_dot(xv, v1_ref[...]).astype(BF16), v2_ref[...]))
        v = v + (vf_ref[...] - v) * gate
    logw = -(jnp.maximum(-w, 0.0) + jnp.log1p(jnp.exp(-jnp.abs(w)))) - 0.5
    lw_out[...] = -jnp.exp(logw)
    kk = k * kk_ref[...]
    ss = _head_sum(kk * kk, hp_ref[...])
    kk = kk * lax.rsqrt(jnp.maximum(ss, 1e-24))
    r_out[...] = r
    k_out[...] = k * (1.0 + (a - 1.0) * ka_ref[...])
    v_out[...] = v
    kn_out[...] = kk
    b_out[...] = kk * a
    g_out[...] = g


def _pad_cols(w, n):
    return jnp.pad(w, ((0, 0), (0, n - w.shape[1])))


def _pad_rows(w, n):
    return jnp.pad(w, ((0, n - w.shape[0]), (0, 0)))


def _rw_pre(x, gn, mix, w_rkv, w0, w1, w2, a0, a1, a2, g1, g2, k_k, k_a, v_first, v_res, S, *, tm=256):
    T, D = x.shape
    has_vres = v_res is not None
    row = lambda i: (i, 0)
    const = lambda i: (0, 0)

    def lora(wa, wb):
        n = -(-wa.shape[1] // LANES) * LANES
        return _pad_cols(wa, n).astype(BF16), _pad_rows(wb, n).astype(BF16)

    w1p, w2p = lora(w1, w2)
    a1p, a2p = lora(a1, a2)
    g1p, g2p = lora(g1, g2)
    args = [x, x, gn, mix, w_rkv[0].astype(BF16), w_rkv[1].astype(BF16), w_rkv[2].astype(BF16),
            w0[None, :], w1p, w2p, a0[None, :], a1p, a2p, g1p, g2p, k_k[None, :], k_a[None, :], _head_sum_matrix()]
    full = lambda a: pl.BlockSpec(a.shape, const)
    in_specs = [pl.BlockSpec((tm, D), row),
                pl.BlockSpec((8, D), lambda i: (jnp.maximum(i * (tm // 8) - 1, 0), 0))]
    in_specs += [full(a) for a in args[2:]]
    if has_vres:
        v0, v1, v2 = v_res
        v1p, v2p = lora(v1, v2)
        extra = [v_first, v0[None, :], v1p, v2p]
        args += extra
        in_specs += [pl.BlockSpec((tm, D), row)] + [full(a) for a in extra[1:]]
    outs = pl.pallas_call(
        functools.partial(_rw_pre_kernel, has_vres=has_vres, seq_tiles=S // tm),
        grid=(T // tm,),
        in_specs=in_specs,
        out_specs=[pl.BlockSpec((tm, D), row)] * 7,
        out_shape=[jax.ShapeDtypeStruct((T, D), F32)] * 7,
        compiler_params=_cparams(("parallel",)),
        name="rw_pre",
    )(*args)
    return outs


def _rw_scan_kernel(r_ref, lw_ref, k_ref, v_ref, kn_ref, b_ref, y_ref, s_ref):
    L = RW_CHUNK

    @pl.when(pl.program_id(1) == 0)
    def _():
        s_ref[...] = jnp.zeros_like(s_ref)

    nb = r_ref.shape[0]

    def rows(ref):
        return jnp.concatenate([ref[s] for s in range(nb)], axis=0)

    lw = rows(lw_ref)
    tr = lax.broadcasted_iota(I32, (nb * L, nb * L), 0)
    tc = lax.broadcasted_iota(I32, (nb * L, nb * L), 1)
    tri = jnp.where((tr >= tc) & (tr // L == tc // L), 1.0, 0.0).astype(BF16)
    hi, mid, lo = _split3(lw)
    lwc = _dot(tri, hi) + _dot(tri, mid) + _dot(tri, lo)
    wl = jnp.concatenate([jnp.broadcast_to(lwc[(s + 1) * L - 1:(s + 1) * L, :], (L, lwc.shape[1]))
                          for s in range(nb)], axis=0)
    w_inc = jnp.exp(lwc)
    w_exc = jnp.exp(lwc - lw)
    w_inv = jnp.exp(-lwc)
    w_end = jnp.exp(wl - lwc)
    w_all = jnp.exp(wl)
    kv = rows(k_ref)
    bv = rows(b_ref)
    vv = rows(v_ref)
    rt = rows(r_ref) * w_inc
    at = -rows(kn_ref) * w_exc
    kt = kv * w_inv
    bt = bv * w_inv
    ke = kv * w_end
    be = bv * w_end

    lane = lax.broadcasted_iota(I32, (1, LANES), 1)
    first = lane < HEAD_DIM

    def sm(x):
        return jnp.concatenate([jnp.where(first, x, 0.0), jnp.where(first, 0.0, x)], axis=0)

    n2 = 2 * L
    ri = lax.broadcasted_iota(I32, (2 * n2, 2 * n2), 0)
    ci = lax.broadcasted_iota(I32, (2 * n2, 2 * n2), 1)
    causal = (ri % L) + ri // n2 > (ci % L)
    bi = lax.broadcasted_iota(I32, (n2, n2), 0) // L
    bj = lax.broadcasted_iota(I32, (n2, n2), 1) // L
    blockdiag = bi == bj
    eye = jnp.where(lax.broadcasted_iota(I32, (n2, n2), 0) == lax.broadcasted_iota(I32, (n2, n2), 1), 1.0, 0.0)

    ents = [(s, p) for s in range(nb) for p in range(N_PAIRS)]
    prs = range(len(ents))
    cut = [(slice(s * L, (s + 1) * L), slice(p * LANES, (p + 1) * LANES)) for s, p in ents]
    vsb = [sm(vv[c]).astype(BF16) for c in cut]
    a_l = [jnp.concatenate([sm(at[c]), sm(rt[c])], axis=0).astype(BF16) for c in cut]
    a_r = [jnp.concatenate([sm(bt[c]), sm(kt[c])], axis=0).astype(BF16) for c in cut]
    m = [jnp.where(causal, _dot_nt(a_l[e], a_r[e]), 0.0) for e in prs]
    s_old = [s_ref[e] for e in prs]
    p12 = [_dot_nt(a_l[e], s_old[e].astype(BF16)) for e in prs]
    pm = [m[e][:n2, :n2].astype(BF16) for e in prs]
    tm = [eye + m[e][:n2, :n2] for e in prs]
    for _ in range(5):
        pm = [_dot(pm[e], pm[e]).astype(BF16) for e in prs]
        tm = [tm[e] + _dot(tm[e].astype(BF16), pm[e]) for e in prs]
    z = [p12[e][:n2] + _dot(m[e][:n2, n2:].astype(BF16), vsb[e]) for e in prs]
    u = [_dot(tm[e].astype(BF16), z[e].astype(BF16)) for e in prs]
    y = [p12[e][n2:] + _dot(m[e][n2:, :].astype(BF16), jnp.concatenate([u[e].astype(BF16), vsb[e]], axis=0))
         for e in prs]
    upd = [_dot_tn(jnp.concatenate([u[e][:L] + u[e][L:], vv[cut[e]]], axis=0).astype(BF16),
                   jnp.concatenate([be[cut[e]], ke[cut[e]]], axis=0).astype(BF16)) for e in prs]
    for e, (s, p) in enumerate(ents):
        y_ref[s, :, cut[e][1]] = y[e][:L] + y[e][L:]
        s_ref[e] = s_old[e] * w_all[s * L:s * L + 1, cut[e][1]] + jnp.where(blockdiag, upd[e], 0.0)


def _rw_scan(r, lw, k, v, kn, b, B, S):
    D = D_MODEL
    L = RW_CHUNK
    nb = 2 if B % 2 == 0 else 1
    spec = pl.BlockSpec((nb, L, D), lambda bb, c: (bb, c, 0))
    args = [a.reshape(B, S, D) for a in (r, lw, k, v, kn, b)]
    y = pl.pallas_call(
        _rw_scan_kernel,
        grid=(B // nb, S // L),
        in_specs=[spec] * 6,
        out_specs=spec,
        out_shape=jax.ShapeDtypeStruct((B, S, D), F32),
        scratch_shapes=[pltpu.VMEM((nb * N_PAIRS, LANES, LANES), F32)],
        compiler_params=_cparams(("parallel", "arbitrary")),
        name="rw_scan",
    )(*args)
    return y.reshape(B * S, D)


def _rw_post_kernel(y_ref, r_ref, k_ref, v_ref, g_ref, rk_ref, lnw_ref, lnb_ref, hp_ref, o_ref):
    hp = hp_ref[...]
    y = y_ref[...]
    inv = 1.0 / HEAD_DIM
    mu = _head_sum(y, hp) * inv
    yc = y - mu
    var = _head_sum(yc * yc, hp) * inv
    yn = yc * lax.rsqrt(var + RW_GN_EPS) * lnw_ref[...] + lnb_ref[...]
    bonus = _head_sum(r_ref[...] * k_ref[...] * rk_ref[...], hp) * v_ref[...]
    o_ref[...] = ((yn + bonus) * g_ref[...]).astype(o_ref.dtype)


def _rw_post(y, r, k, v, g, r_k, ln_w, ln_b, *, tm=256):
    T, D = y.shape
    row = lambda i: (i, 0)
    const = lambda i: (0, 0)
    return pl.pallas_call(
        _rw_post_kernel,
        grid=(T // tm,),
        in_specs=[pl.BlockSpec((tm, D), row)] * 5 + [pl.BlockSpec((1, D), const)] * 3
        + [pl.BlockSpec((2 * HS_W, HS_W), const)],
        out_specs=pl.BlockSpec((tm, D), row),
        out_shape=jax.ShapeDtypeStruct((T, D), BF16),
        compiler_params=_cparams(("parallel",)),
        name="rw_post",
    )(y, r, k, v, g, r_k.reshape(1, D), ln_w[None, :], ln_b[None, :], _head_sum_matrix())


def _rwkv_mixer(x, gn, B, S, mix, w_rkv, w0, w1, w2, a0, a1, a2, g1, g2, k_k, k_a, r_k, ln_w, ln_b, v_first, v_res):
    r, lw, k, v, kn, b, g = _rw_pre(x, gn, mix, w_rkv, w0, w1, w2, a0, a1, a2, g1, g2, k_k, k_a, v_first, v_res, S)
    y = _rw_scan(r, lw, k, v, kn, b, B, S)
    return _rw_post(y, r, k, v, g, r_k, ln_w, ln_b), v


def _sb_mixer(x, gn, B, S, w_qkv):
    qkv = _norm_proj(x, gn, w_qkv.astype(BF16))
    return _sb_attention(qkv.reshape(B, S, 3 * D_MODEL), B, S).reshape(B * S, D_MODEL)


def _dsa_mixer(x, gn, B, S, w_in, q_norm, k_norm):
    qkv, qcat, kcat, wi, kmx = _dsa_proj(x, gn, w_in, q_norm, k_norm, B, S)
    o = _dsa_attention(qkv.reshape(B, S, 3 * D_MODEL), qcat, kcat, wi, kmx, B, S)
    return o.reshape(B * S, D_MODEL)


def kernel(x, norm_mix, norm_ffn, ffn_w_gu, ffn_w_down, rw_mix, rw_w_rkv, rw_w0, rw_w1, rw_w2, rw_a0, rw_a1, rw_a2, rw_g1, rw_g2, rw_v0, rw_v1, rw_v2, rw_k_k, rw_k_a, rw_r_k, rw_ln_w, rw_ln_b, rw_w_out, sb_w_qkv, sb_w_out, ds_w_in, ds_q_norm, ds_k_norm, ds_w_out):
    B, S, D = x.shape
    depth = norm_mix.shape[0]
    xf = x.reshape(B * S, D)
    v_first = None
    for i in range(depth):
        kind, j = i % 3, i // 3
        gn = norm_mix[i][None, :]
        if kind == 0:
            v_res = None if j == 0 else (rw_v0[j - 1], rw_v1[j - 1], rw_v2[j - 1])
            o, v_layer = _rwkv_mixer(xf, gn, B, S, rw_mix[j], rw_w_rkv[j], rw_w0[j], rw_w1[j], rw_w2[j],
                                     rw_a0[j], rw_a1[j], rw_a2[j], rw_g1[j], rw_g2[j], rw_k_k[j], rw_k_a[j],
                                     rw_r_k[j], rw_ln_w[j], rw_ln_b[j], v_first, v_res)
            if j == 0:
                v_first = v_layer
            w_out = rw_w_out[j]
        elif kind == 1:
            o = _sb_mixer(xf, gn, B, S, sb_w_qkv[j])
            w_out = sb_w_out[j]
        else:
            o = _dsa_mixer(xf, gn, B, S, ds_w_in[j], ds_q_norm[j], ds_k_norm[j])
            w_out = ds_w_out[j]
        xf = _ffn_layer(xf, o, w_out.astype(BF16), norm_ffn[i][None, :], ffn_w_gu[i].astype(BF16),
                        ffn_w_down[i].astype(BF16))
    return xf.reshape(B, S, D)
```

```python
import functools
import math

import jax
import jax.numpy as jnp
from jax import lax
from jax.experimental import pallas as pl
from jax.experimental.pallas import tpu as pltpu

F32 = jnp.float32
BF16 = jnp.bfloat16
I32 = jnp.int32

D_MODEL = 1024
HEAD_DIM = 64
N_HEADS = D_MODEL // HEAD_DIM
N_PAIRS = N_HEADS // 2
LANES = 128
RMS_EPS = 1e-6
ROPE_THETA = 10000.0
RW_GN_EPS = HEAD_DIM * 1e-5
RW_CHUNK = 64
DS_TOPK_MAX = 256
DS_IDX_HEADS = 8
DS_IDX_DIM = 64
DS_CHUNK = 64
NEG_BIG = -1e30
SB_DEAD_LOG = -105.0
DS_SAFE_LOGIT = 35.0
VMEM_LIMIT = 56 * 1024 * 1024


def _cparams(sem):
    return pltpu.CompilerParams(dimension_semantics=sem, vmem_limit_bytes=VMEM_LIMIT)


def _dot(a, b):
    return jnp.dot(a, b, preferred_element_type=F32)


def _dot_nt(a, b):
    return lax.dot_general(a, b, (((1,), (1,)), ((), ())), preferred_element_type=F32)


def _dot_tn(a, b):
    return lax.dot_general(a, b, (((0,), (0,)), ((), ())), preferred_element_type=F32)


def _split2(x):
    hi = x.astype(BF16)
    lo = (x - hi.astype(F32)).astype(BF16)
    return hi, lo


def _split3(x):
    hi = x.astype(BF16)
    r1 = x - hi.astype(F32)
    mid = r1.astype(BF16)
    lo = (r1 - mid.astype(F32)).astype(BF16)
    return hi, mid, lo


HS_W = 256


def _head_sum(x, hp):
    hi, lo = _split2(x)
    outs = []
    for s in range(x.shape[1] // HS_W):
        sl = slice(s * HS_W, (s + 1) * HS_W)
        outs.append(_dot(jnp.concatenate([hi[:, sl], lo[:, sl]], axis=1), hp))
    return jnp.concatenate(outs, axis=1)


def _rms(x, g):
    ms = jnp.mean(x * x, axis=-1, keepdims=True)
    return x * lax.rsqrt(ms + RMS_EPS) * g


def _ffn_kernel(x_ref, o_ref, wo_ref, g_ref, wgu_ref, wd_ref, out_ref, x1_ref, hn_ref, acc_ref, act_ref, *, tf):
    F = wd_ref.shape[0]
    nf = F // tf
    x1 = x_ref[...] + _dot(o_ref[...], wo_ref[...])
    x1_ref[...] = x1
    hn_ref[...] = _rms(x1, g_ref[...]).astype(BF16)
    for f in range(nf + 1):
        if f < nf:
            hn = hn_ref[...]
            g = _dot(hn, wgu_ref[:, f * tf:(f + 1) * tf])
            u = _dot(hn, wgu_ref[:, F + f * tf:F + (f + 1) * tf])
        if f > 0:
            d = _dot(act_ref[(f - 1) % 2], wd_ref[(f - 1) * tf:f * tf, :])
            if f == 1:
                acc_ref[...] = d
            else:
                acc_ref[...] += d
        if f < nf:
            act_ref[f % 2] = (g * jax.nn.sigmoid(g) * u).astype(BF16)
    out_ref[...] = x1_ref[...] + acc_ref[...]


def _ffn_layer(x, o, w_out, g_ffn, w_gu, w_down, *, tm=512, tf=256):
    T, D = x.shape
    F = w_down.shape[0]
    assert T % tm == 0 and F % tf == 0
    row = lambda i: (i, 0)
    const = lambda i: (0, 0)
    resident = lambda a: pl.BlockSpec(a.shape, const, pipeline_mode=pl.Buffered(1))
    return pl.pallas_call(
        functools.partial(_ffn_kernel, tf=tf),
        grid=(T // tm,),
        in_specs=[
            pl.BlockSpec((tm, D), row),
            pl.BlockSpec((tm, D), row),
            resident(w_out),
            pl.BlockSpec((1, D), const),
            resident(w_gu),
            resident(w_down),
        ],
        out_specs=pl.BlockSpec((tm, D), row),
        out_shape=jax.ShapeDtypeStruct((T, D), F32),
        scratch_shapes=[pltpu.VMEM((tm, D), F32), pltpu.VMEM((tm, D), BF16), pltpu.VMEM((tm, D), F32),
                        pltpu.VMEM((2, tm, tf), BF16)],
        compiler_params=_cparams(("parallel",)),
        name="ffn",
    )(x, o, w_out, g_ffn, w_gu, w_down)


def _norm_proj_kernel(x_ref, g_ref, w_ref, out_ref, hn_ref):
    @pl.when(pl.program_id(1) == 0)
    def _():
        hn_ref[...] = _rms(x_ref[...], g_ref[...]).astype(BF16)

    out_ref[...] = _dot(hn_ref[...], w_ref[...]).astype(out_ref.dtype)


def _norm_proj(x, g, w, *, tm=512, tn=1024):
    T, D = x.shape
    N = w.shape[1]
    return pl.pallas_call(
        _norm_proj_kernel,
        grid=(T // tm, N // tn),
        in_specs=[
            pl.BlockSpec((tm, D), lambda i, j: (i, 0)),
            pl.BlockSpec((1, D), lambda i, j: (0, 0)),
            pl.BlockSpec((D, tn), lambda i, j: (0, j)),
        ],
        out_specs=pl.BlockSpec((tm, tn), lambda i, j: (i, j)),
        out_shape=jax.ShapeDtypeStruct((T, N), BF16),
        scratch_shapes=[pltpu.VMEM((tm, D), BF16)],
        compiler_params=_cparams(("parallel", "arbitrary")),
        name="norm_proj",
    )(x, g, w)


def _sb_kernel(q_ref, k_ref, v_ref, suo_ref, o_ref, c_ref, a_ref, zs_ref, hl_ref, *, tq, tk):
    qi = pl.program_id(2)
    lane = lax.broadcasted_iota(I32, (1, LANES), 1)
    first = lane < HEAD_DIM
    q = q_ref[...] * jnp.asarray(HEAD_DIM ** -0.5, BF16)
    zero = jnp.zeros_like(q)
    qh = (jnp.where(first, q, zero), jnp.where(first, zero, q))
    suo = suo_ref[...]
    c_ref[...] = jnp.zeros_like(c_ref)
    a_ref[...] = jnp.zeros_like(a_ref)
    qpos = qi * tq + lax.broadcasted_iota(I32, (tq, 1), 0)
    nsub = tq // tk

    def a_dots(j):
        kb = k_ref[pl.ds(pl.multiple_of(j * tq, tq), tq), :]
        return [_dot_nt(qh[h], kb) for h in range(2)]

    def a_rest(j, zz, slot, masked):
        if masked:
            earlier = (j * tq + lax.broadcasted_iota(I32, (1, tq), 1)) < qpos
        for h in range(2):
            z = zz[h]
            sp = jnp.maximum(z, 0.0) + jnp.log(1.0 + jnp.exp(-jnp.abs(z)))
            zs = z - sp
            if masked:
                sp = jnp.where(earlier, sp, 0.0)
                zs = jnp.where(earlier, zs, NEG_BIG)
            zs_ref[slot, h] = zs
            for u in range(nsub):
                hi, lo = _split2(sp[:, u * tk:(u + 1) * tk])
                hl_ref[slot, h, :, 2 * u * tk:(2 * u + 1) * tk] = hi
                hl_ref[slot, h, :, (2 * u + 1) * tk:(2 * u + 2) * tk] = lo

    def b_dots(slot):
        return [[_dot(hl_ref[slot, h, :, 2 * u * tk:(2 * u + 2) * tk], suo) for u in range(nsub)] for h in range(2)]

    def b_rest(j, slot, cs):
        vb = v_ref[pl.ds(pl.multiple_of(j * tq, tq), tq), :]
        zv = jnp.zeros_like(vb)
        vb2 = jnp.concatenate([jnp.where(first, vb, zv), jnp.where(first, zv, vb)], axis=0)
        att = []
        for h in range(2):
            c = c_ref[h]
            parts = [None] * nsub
            for u in reversed(range(nsub)):
                a = jnp.exp(zs_ref[slot, h, :, u * tk:(u + 1) * tk] + cs[h][u][:, :tk] + c)
                parts[u] = a.astype(BF16)
                c = c + cs[h][u][:, tk:]
            c_ref[h] = c
            att += parts
        a_ref[...] += _dot(jnp.concatenate(att, axis=1), vb2)

    def tile(j, masked):
        a_rest(j, a_dots(j), 0, masked)
        b_rest(j, 0, b_dots(0))

    def c_max():
        return jnp.max(jnp.maximum(c_ref[0], c_ref[1]))

    tile(qi, True)

    def keep_going(carry):
        t, cm = carry
        return jnp.logical_and(t < qi, cm > SB_DEAD_LOG)

    def step(carry):
        t, _ = carry
        tile(qi - 1 - t, False)
        return t + 1, c_max()

    lax.while_loop(keep_going, step, (jnp.int32(0), c_max()))

    o_ref[...] = a_ref[...].astype(o_ref.dtype)


def _sb_attention(qkv, B, S, *, tq=256, tk=128):
    D = D_MODEL
    r = lax.broadcasted_iota(I32, (tk, 2 * tk), 0)
    c = lax.broadcasted_iota(I32, (tk, 2 * tk), 1)
    suo = jnp.where((c >= tk) | (r > c), -1.0, 0.0).astype(BF16)
    suo = jnp.concatenate([suo, suo], axis=0)
    kern = functools.partial(_sb_kernel, tq=tq, tk=tk)
    return pl.pallas_call(
        kern,
        grid=(B, N_PAIRS, S // tq),
        in_specs=[
            pl.BlockSpec((None, tq, LANES), lambda b, p, i: (b, i, p)),
            pl.BlockSpec((None, S, LANES), lambda b, p, i: (b, 0, N_PAIRS + p)),
            pl.BlockSpec((None, S, LANES), lambda b, p, i: (b, 0, 2 * N_PAIRS + p)),
            pl.BlockSpec((2 * tk, 2 * tk), lambda b, p, i: (0, 0)),
        ],
        out_specs=pl.BlockSpec((None, tq, LANES), lambda b, p, i: (b, i, p)),
        out_shape=jax.ShapeDtypeStruct((B, S, D), BF16),
        scratch_shapes=[pltpu.VMEM((2, tq, tk), F32), pltpu.VMEM((tq, LANES), F32),
                        pltpu.VMEM((1, 2, tq, tq), F32), pltpu.VMEM((1, 2, tq, 2 * tq), BF16)],
        compiler_params=_cparams(("parallel", "parallel", "arbitrary")),
        name="sb_attn",
    )(qkv, qkv, qkv, suo)


def _rope_slabs(y, cos, sin):
    lane = lax.broadcasted_iota(I32, (1, LANES), 1)
    first = (lane % HEAD_DIM) < (HEAD_DIM // 2)
    outs = []
    for s in range(y.shape[1] // LANES):
        ys = y[:, s * LANES:(s + 1) * LANES]
        rot = jnp.where(first, pltpu.roll(ys, LANES - HEAD_DIM // 2, axis=1), pltpu.roll(ys, HEAD_DIM // 2, axis=1))
        outs.append(ys * cos[:, s * LANES:(s + 1) * LANES] + rot * sin[:, s * LANES:(s + 1) * LANES])
    return jnp.concatenate(outs, axis=1)


def _dsa_proj_kernel(x_ref, g_ref, w_ref, wih_ref, wil_ref, hp_ref, qn_ref, kn_ref, cos_ref, sin_ref,
                     qkv_ref, qcat_ref, kcat_ref, wi_ref, kmx_ref, hh_ref, hl_ref, *, wi_scale):
    j = pl.program_id(1)
    nq = DS_IDX_HEADS * DS_IDX_DIM

    @pl.when(j == 0)
    def _():
        hn = _rms(x_ref[...], g_ref[...])
        hh, hl = _split2(hn)
        hh_ref[...] = hh
        hl_ref[...] = hl
        wih = wih_ref[...]
        idx = _dot(hh, wih) + _dot(hl, wih) + _dot(hh, wil_ref[...])
        cos = cos_ref[...]
        sin = sin_ref[...]
        lane = lax.broadcasted_iota(I32, (1, LANES), 1)
        first = lane < DS_IDX_DIM

        def hi_lo(x):
            hi = x.astype(BF16).astype(F32)
            return hi, x - hi

        qi = _rope_slabs(idx[:, :nq], cos[:, :nq], sin[:, :nq])
        qhi, qlo = hi_lo(qi)
        for h in range(DS_IDX_HEADS):
            sl = slice((h // 2) * LANES, (h // 2 + 1) * LANES)
            if h % 2 == 0:
                own_hi = qhi[:, sl]
                a = jnp.where(first, own_hi, pltpu.roll(qlo[:, sl], DS_IDX_DIM, axis=1))
            else:
                own_hi = pltpu.roll(qhi[:, sl], DS_IDX_DIM, axis=1)
                a = jnp.where(first, own_hi, qlo[:, sl])
            b = jnp.where(first, own_hi, 0.0)
            qcat_ref[:, 2 * h * LANES:(2 * h + 1) * LANES] = a.astype(BF16)
            qcat_ref[:, (2 * h + 1) * LANES:(2 * h + 2) * LANES] = b.astype(BF16)
        tail = idx[:, nq:nq + LANES]
        ki = jnp.where(first, _rope_slabs(tail, cos[:, :LANES], sin[:, :LANES]), 0.0)
        khi, klo = hi_lo(ki)
        kcat_ref[:, :LANES] = (khi + pltpu.roll(khi, DS_IDX_DIM, axis=1)).astype(BF16)
        kcat_ref[:, LANES:] = klo.astype(BF16)
        wi_ref[...] = tail * wi_scale

    y = _dot(hh_ref[...], w_ref[...])

    def normed(gn_ref, scale):
        ms = _head_sum(y * y, hp_ref[...]) * (1.0 / HEAD_DIM)
        yn = y * lax.rsqrt(ms + RMS_EPS) * gn_ref[...]
        out = _rope_slabs(yn, cos_ref[...], sin_ref[...])
        return (out * scale).astype(qkv_ref.dtype)

    @pl.when(j == 0)
    def _():
        qkv_ref[...] = normed(qn_ref, HEAD_DIM ** -0.5)

    @pl.when(j == 1)
    def _():
        kb = normed(kn_ref, 1.0)
        qkv_ref[...] = kb
        kf = kb.astype(F32)
        kmx_ref[...] = jnp.max(_head_sum(kf * kf, hp_ref[...]), axis=0, keepdims=True)

    @pl.when(j == 2)
    def _():
        qkv_ref[...] = y.astype(qkv_ref.dtype)


def _head_sum_matrix():
    r = (lax.broadcasted_iota(I32, (2 * HS_W, HS_W), 0) % HS_W) // HEAD_DIM
    c = lax.broadcasted_iota(I32, (2 * HS_W, HS_W), 1) // HEAD_DIM
    return jnp.where(r == c, 1.0, 0.0).astype(BF16)


def _rope_tables(S):
    half = HEAD_DIM // 2
    inv = 1.0 / (ROPE_THETA ** (jnp.arange(half, dtype=F32) / half))
    ang = jnp.arange(S, dtype=F32)[:, None] * inv[None, :]
    cos = jnp.concatenate([jnp.cos(ang), jnp.cos(ang)], axis=1)
    sin = jnp.concatenate([-jnp.sin(ang), jnp.sin(ang)], axis=1)
    return jnp.tile(cos, (1, N_HEADS)), jnp.tile(sin, (1, N_HEADS))


def _dsa_proj(x, g, w_in, q_norm, k_norm, B, S, *, tm=256):
    T, D = x.shape
    c3 = 3 * D
    nq = DS_IDX_HEADS * DS_IDX_DIM
    n_idx = nq + DS_IDX_DIM + DS_IDX_HEADS
    w_qkv = w_in[:, :c3].astype(BF16)
    w_idx = jnp.pad(w_in[:, c3:], ((0, 0), (0, nq + LANES - n_idx)))
    wih = w_idx.astype(BF16)
    wil = (w_idx - wih.astype(F32)).astype(BF16)
    cos, sin = _rope_tables(S)
    qn = jnp.tile(q_norm, N_HEADS)[None, :]
    kn = jnp.tile(k_norm, N_HEADS)[None, :]
    wi_scale = DS_IDX_HEADS ** -0.5 * DS_IDX_DIM ** -0.5
    nsb = S // tm
    row = lambda i, j: (i, 0)
    const = lambda i, j: (0, 0)
    pos = lambda i, j: (i % nsb, 0)
    return pl.pallas_call(
        functools.partial(_dsa_proj_kernel, wi_scale=wi_scale),
        grid=(T // tm, 3),
        in_specs=[
            pl.BlockSpec((tm, D), row),
            pl.BlockSpec((1, D), const),
            pl.BlockSpec((D, D), lambda i, j: (0, j)),
            pl.BlockSpec((D, nq + LANES), const),
            pl.BlockSpec((D, nq + LANES), const),
            pl.BlockSpec((2 * HS_W, HS_W), const),
            pl.BlockSpec((1, D), const),
            pl.BlockSpec((1, D), const),
            pl.BlockSpec((tm, D), pos),
            pl.BlockSpec((tm, D), pos),
        ],
        out_specs=[
            pl.BlockSpec((tm, D), lambda i, j: (i, j)),
            pl.BlockSpec((tm, 2 * LANES * DS_IDX_HEADS), row),
            pl.BlockSpec((tm, 2 * LANES), row),
            pl.BlockSpec((tm, LANES), row),
            pl.BlockSpec((None, 1, D), lambda i, j: (i, 0, 0)),
        ],
        out_shape=[
            jax.ShapeDtypeStruct((T, c3), BF16),
            jax.ShapeDtypeStruct((T, 2 * LANES * DS_IDX_HEADS), BF16),
            jax.ShapeDtypeStruct((T, 2 * LANES), BF16),
            jax.ShapeDtypeStruct((T, LANES), F32),
            jax.ShapeDtypeStruct((T // tm, 1, D), F32),
        ],
        scratch_shapes=[pltpu.VMEM((tm, D), BF16), pltpu.VMEM((tm, D), BF16)],
        compiler_params=_cparams(("parallel", "arbitrary")),
        name="dsa_proj",
    )(x, g, w_qkv, wih, wil, _head_sum_matrix(), qn, kn, cos, sin)


def _dsa_attn_kernel(q_ref, k_ref, v_ref, qcat_ref, kcat_ref, wi_ref, kmx_ref, tri_ref, o_ref,
                     keys_ref, bias_ref, thr_ref, need_ref, eqb_ref, m_ref, mx_ref, acc_ref,
                     *, tq, ta, rb, n_sel):
    i = pl.program_id(1)
    p = pl.program_id(2)
    lane = lax.broadcasted_iota(I32, (1, LANES), 1)
    row = lax.broadcasted_iota(I32, (tq, 1), 0)
    vis_row = i * tq + (row // DS_CHUNK + 1) * DS_CHUNK
    vis_tile = (i + 1) * tq
    n_a = (vis_tile + ta - 1) // ta

    @pl.when(p == 0)
    def _select():
        wi = wi_ref[...]

        def score_blk(c, _):
            cs = pl.multiple_of(c * ta, ta)
            kc = kcat_ref[pl.ds(cs, ta), :]
            idxs = [_dot_nt(qcat_ref[:, 2 * h * LANES:(2 * h + 2) * LANES], kc) for h in range(DS_IDX_HEADS)]
            sc = jnp.zeros((tq, ta), F32)
            for h in range(DS_IDX_HEADS):
                sc = sc + wi[:, DS_IDX_DIM + h:DS_IDX_DIM + h + 1] * jnp.maximum(idxs[h], 0.0)
            sc = jnp.where(sc == 0.0, 0.0, sc)
            kpos = cs + lax.broadcasted_iota(I32, (1, ta), 1)
            sc = jnp.where(kpos < vis_row, sc, -jnp.inf)
            bits = lax.bitcast_convert_type(sc, I32)
            keys_ref[:, pl.ds(cs, ta)] = jnp.where(bits < 0, bits ^ jnp.int32(0x7FFFFFFF), bits)
            return 0

        int_min = jnp.iinfo(jnp.int32).min
        lax.fori_loop(0, n_a, score_blk, 0)

        @pl.when(n_a % 2 == 1)
        def _():
            keys_ref[:, pl.ds(pl.multiple_of(n_a * ta, ta), ta)] = jnp.full((tq, ta), int_min, I32)

        nsweep = 2 * ta // LANES
        groups = [slice(g * rb, (g + 1) * rb) for g in range(tq // rb)]

        def count(cmps, strict):
            outs = []
            for g0 in range(0, len(groups), 2):
                def blk(c, parts, g0=g0):
                    cs = pl.multiple_of(c * 2 * ta, 2 * ta)
                    parts = list(parts)
                    for s in range(nsweep):
                        for d in range(2):
                            kk = keys_ref[groups[g0 + d], pl.ds(cs + s * LANES, LANES)]
                            hit = (kk > cmps[g0 + d]) if strict else (kk >= cmps[g0 + d])
                            parts[d] = parts[d] + jnp.where(hit, 1, 0).astype(I32)
                    return tuple(parts)
                zero = jnp.zeros((rb, LANES), I32)
                outs += list(lax.fori_loop(0, (n_a + 1) // 2, blk, (zero, zero)))
            return [jnp.sum(pt, axis=1, keepdims=True) for pt in outs]

        thr_ref[...] = jnp.full(thr_ref.shape, int_min, I32)

        def bit_step(t, _):
            bit = lax.shift_left(jnp.int32(1), 31 - t)
            cands = [thr_ref[g, :] + bit for g in groups]
            cnts = count(cands, False)
            for g, cand, cnt in zip(groups, cands, cnts):
                thr_ref[g, :] = jnp.where(cnt >= n_sel, cand, thr_ref[g, :])
            return 0

        lax.fori_loop(0, 32, bit_step, 0)
        nrep = ta // LANES
        n_ge = count([thr_ref[g, :] for g in groups], False)
        tied = sum(jnp.max(jnp.abs(cnt - n_sel)) for cnt in n_ge) > 0

        @pl.when(jnp.logical_not(tied))
        def _():
            def bias_blk(c, _):
                cs = pl.multiple_of(c * ta, ta)
                kk = keys_ref[:, pl.ds(cs, ta)]
                thr = jnp.concatenate([thr_ref[...]] * nrep, axis=1)
                kpos = cs + lax.broadcasted_iota(I32, (1, ta), 1)
                bias_ref[:, pl.ds(cs, ta)] = jnp.where(kpos < vis_row, jnp.where(kk >= thr, 0.0, NEG_BIG), NEG_BIG)
                return 0

            lax.fori_loop(0, n_a, bias_blk, 0)

        @pl.when(tied)
        def _():
            for g, cnt in zip(groups, count([thr_ref[g, :] for g in groups], True)):
                need_ref[g, :] = jnp.broadcast_to((n_sel - cnt).astype(F32), (rb, LANES))
            tri = tri_ref[...]

            def bias_blk(c, _):
                cs = pl.multiple_of(c * ta, ta)
                kk = keys_ref[:, pl.ds(cs, ta)]
                thr = jnp.concatenate([thr_ref[...]] * nrep, axis=1)
                eqf = jnp.where(kk == thr, 1.0, 0.0)
                ranks = _dot(eqf.astype(BF16), tri)
                rank = jnp.concatenate([eqb_ref[...]] * nrep, axis=1) + ranks[:, :ta]
                need = jnp.concatenate([need_ref[...]] * nrep, axis=1)
                sel = jnp.where(kk > thr, 1.0, jnp.where(rank < need, eqf, 0.0))
                kpos = cs + lax.broadcasted_iota(I32, (1, ta), 1)
                sel = jnp.where(kpos < vis_row, sel, 0.0)
                bias_ref[:, pl.ds(cs, ta)] = (sel - 1.0) * (-NEG_BIG)
                eqb_ref[...] += ranks[:, ta:]
                return 0

            eqb_ref[...] = jnp.zeros_like(eqb_ref)
            lax.fori_loop(0, n_a, bias_blk, 0)

        @pl.when(n_a % 2 == 1)
        def _():
            bias_ref[:, pl.ds(pl.multiple_of(n_a * ta, ta), ta)] = jnp.full((tq, ta), NEG_BIG, F32)

    q = q_ref[...]
    zero = jnp.zeros_like(q)
    first = lane < HEAD_DIM
    qh2 = (jnp.where(first, q, zero), jnp.where(first, zero, q))
    n_pair = (n_a + 1) // 2

    trow = lax.broadcasted_iota(I32, (kmx_ref.shape[0], 1), 0)
    kmax2 = jnp.max(jnp.where(trow <= i, kmx_ref[...], 0.0), axis=0, keepdims=True)
    hr = lax.broadcasted_iota(I32, (LANES, LANES), 0) // HEAD_DIM
    hc = lax.broadcasted_iota(I32, (LANES, LANES), 1) // HEAD_DIM
    qf = q.astype(F32)
    q2 = _dot((qf * qf).astype(BF16), jnp.where(hr == hc, 1.0, 0.0).astype(BF16))
    bound2 = 1.05 * jnp.max(q2 * kmax2)
    m_ref[...] = jnp.zeros_like(m_ref)

    @pl.when(bound2 > DS_SAFE_LOGIT * DS_SAFE_LOGIT)
    def _row_max():
        mx_ref[...] = jnp.full(mx_ref.shape, NEG_BIG, F32)

        def max_blk(c2, _):
            for u in range(2):
                cs = pl.multiple_of((2 * c2 + u) * ta, ta)
                kb = k_ref[pl.ds(cs, ta), :]
                bias = bias_ref[:, pl.ds(cs, ta)]
                for h in range(2):
                    mx_ref[u, h] = jnp.maximum(mx_ref[u, h], _dot_nt(qh2[h], kb) + bias)
            return 0

        lax.fori_loop(0, n_pair, max_blk, 0)
        for h in range(2):
            m_ref[h] = jnp.broadcast_to(jnp.max(jnp.maximum(mx_ref[0, h], mx_ref[1, h]), axis=1, keepdims=True),
                                        (tq, LANES))

    ones_h = [jnp.broadcast_to(jnp.where(first, a, b), (ta, LANES)).astype(BF16) for a, b in ((1.0, 0.0), (0.0, 1.0))]
    acc_ref[...] = jnp.zeros_like(acc_ref)

    def attn_blocks(c0, nblk):
        css = [pl.multiple_of((c0 + u) * ta, ta) for u in range(nblk)]
        logits = [[_dot_nt(qh2[h], k_ref[pl.ds(cs, ta), :]) for h in range(2)] for cs in css]
        out = acc_ref[...]
        for u, cs in enumerate(css):
            vb = v_ref[pl.ds(cs, ta), :]
            zv = jnp.zeros_like(vb)
            vs = [jnp.concatenate([jnp.where(first, vb, zv), ones_h[0]], axis=1),
                  jnp.concatenate([jnp.where(first, zv, vb), ones_h[1]], axis=1)]
            bias = bias_ref[:, pl.ds(cs, ta)]
            for h in range(2):
                pr = jnp.exp(logits[u][h] + bias - m_ref[h][:, :1]).astype(BF16)
                out = out + _dot(pr, vs[h])
        acc_ref[...] = out

    n_quad = n_a // 4

    def quad(c4, _):
        attn_blocks(4 * c4, 4)
        return 0

    lax.fori_loop(0, n_quad, quad, 0)
    rest = n_a - 4 * n_quad

    @pl.when(rest > 0)
    def _():
        attn_blocks(4 * n_quad, 2)

    @pl.when(rest == 3)
    def _():
        attn_blocks(4 * n_quad + 2, 2)
    acc = acc_ref[...]
    o_ref[...] = (acc[:, :LANES] / acc[:, LANES:]).astype(o_ref.dtype)


def _dsa_attention(qkv, qcat, kcat, wi, kmx, B, S, *, tq=256, ta=512, rb=64):
    D = D_MODEL
    n_sel = min(DS_TOPK_MAX, S // 4)
    assert S % (2 * ta) == 0 and S // LANES <= 256 and kmx.shape[0] * tq == B * S
    nqc = 2 * LANES * DS_IDX_HEADS
    r = lax.broadcasted_iota(I32, (ta, ta + LANES), 0)
    c = lax.broadcasted_iota(I32, (ta, ta + LANES), 1)
    tri = jnp.where((r < c) | (c >= ta), 1.0, 0.0).astype(BF16)
    kern = functools.partial(_dsa_attn_kernel, tq=tq, ta=ta, rb=rb, n_sel=n_sel)
    s_pad = S
    return pl.pallas_call(
        kern,
        grid=(B, S // tq, N_PAIRS),
        in_specs=[
            pl.BlockSpec((None, tq, LANES), lambda b, i, p: (b, i, p)),
            pl.BlockSpec((None, S, LANES), lambda b, i, p: (b, 0, N_PAIRS + p)),
            pl.BlockSpec((None, S, LANES), lambda b, i, p: (b, 0, 2 * N_PAIRS + p)),
            pl.BlockSpec((None, tq, nqc), lambda b, i, p: (b, i, 0)),
            pl.BlockSpec((None, S, 2 * LANES), lambda b, i, p: (b, 0, 0)),
            pl.BlockSpec((None, tq, LANES), lambda b, i, p: (b, i, 0)),
            pl.BlockSpec((None, S // tq, LANES), lambda b, i, p: (b, 0, p)),
            pl.BlockSpec((ta, ta + LANES), lambda b, i, p: (0, 0)),
        ],
        out_specs=pl.BlockSpec((None, tq, LANES), lambda b, i, p: (b, i, p)),
        out_shape=jax.ShapeDtypeStruct((B, S, D), BF16),
        scratch_shapes=[
            pltpu.VMEM((tq, s_pad), I32),
            pltpu.VMEM((tq, s_pad), F32),
            pltpu.VMEM((tq, LANES), I32),
            pltpu.VMEM((tq, LANES), F32),
            pltpu.VMEM((tq, LANES), F32),
            pltpu.VMEM((2, tq, LANES), F32),
            pltpu.VMEM((2, 2, tq, ta), F32),
            pltpu.VMEM((tq, 2 * LANES), F32),
        ],
        compiler_params=_cparams(("parallel", "arbitrary", "arbitrary")),
        name="dsa_attn",
    )(qkv, qkv, qkv, qcat.reshape(B, S, nqc), kcat.reshape(B, S, 2 * LANES), wi.reshape(B, S, LANES),
      kmx.reshape(B, S // tq, D), tri)


def _rw_pre_kernel(*refs, has_vres, seq_tiles):
    if has_vres:
        (x_ref, xp_ref, gn_ref, mix_ref, wr_ref, wk_ref, wv_ref, w0_ref, w1_ref, w2_ref, a0_ref, a1_ref, a2_ref,
         g1_ref, g2_ref, kk_ref, ka_ref, hp_ref, vf_ref, v0_ref, v1_ref, v2_ref,
         r_out, lw_out, k_out, v_out, kn_out, b_out, g_out) = refs
    else:
        (x_ref, xp_ref, gn_ref, mix_ref, wr_ref, wk_ref, wv_ref, w0_ref, w1_ref, w2_ref, a0_ref, a1_ref, a2_ref,
         g1_ref, g2_ref, kk_ref, ka_ref, hp_ref,
         r_out, lw_out, k_out, v_out, kn_out, b_out, g_out) = refs
    i = pl.program_id(0)
    gn = gn_ref[...]
    h = _rms(x_ref[...], gn)
    tm = h.shape[0]
    hprev = _rms(xp_ref[...], gn)[7:8, :]
    hprev = jnp.where(i % seq_tiles == 0, 0.0, hprev)
    rowi = lax.broadcasted_iota(I32, (tm, 1), 0)
    sh = jnp.where(rowi == 0, hprev, pltpu.roll(h, 1, axis=0))
    dlt = sh - h
    mix = mix_ref[...]

    def stream(j):
        return (h + dlt * mix[j:j + 1, :]).astype(BF16)

    r = _dot(stream(0), wr_ref[...])
    k = _dot(stream(1), wk_ref[...])
    xv = stream(2)
    v = _dot(xv, wv_ref[...])
    w = w0_ref[...] + _dot(jnp.tanh(_dot(stream(3), w1_ref[...])).astype(BF16), w2_ref[...])
    a = jax.nn.sigmoid(a0_ref[...] + _dot(_dot(stream(4), a1_ref[...]).astype(BF16), a2_ref[...]))
    g = _dot(jax.nn.sigmoid(_dot(stream(5), g1_ref[...])).astype(BF16), g2_ref[...])
    if has_vres:
        gate = jax.nn.sigmoid(v0_ref[...] + _dot(_dot(xv, v1_ref[...]).astype(BF16), v2_ref[...]))
        v = v + (vf_ref[...] - v) * gate
    logw = -(jnp.maximum(-w, 0.0) + jnp.log1p(jnp.exp(-jnp.abs(w)))) - 0.5
    lw_out[...] = -jnp.exp(logw)
    kk = k * kk_ref[...]
    ss = _head_sum(kk * kk, hp_ref[...])
    kk = kk * lax.rsqrt(jnp.maximum(ss, 1e-24))
    r_out[...] = r
    k_out[...] = k * (1.0 + (a - 1.0) * ka_ref[...])
    v_out[...] = v
    kn_out[...] = kk
    b_out[...] = kk * a
    g_out[...] = g


def _pad_cols(w, n):
    return jnp.pad(w, ((0, 0), (0, n - w.shape[1])))


def _pad_rows(w, n):
    return jnp.pad(w, ((0, n - w.shape[0]), (0, 0)))


def _rw_pre(x, gn, mix, w_rkv, w0, w1, w2, a0, a1, a2, g1, g2, k_k, k_a, v_first, v_res, S, *, tm=256):
    T, D = x.shape
    has_vres = v_res is not None
    row = lambda i: (i, 0)
    const = lambda i: (0, 0)

    def lora(wa, wb):
        n = -(-wa.shape[1] // LANES) * LANES
        return _pad_cols(wa, n).astype(BF16), _pad_rows(wb, n).astype(BF16)

    w1p, w2p = lora(w1, w2)
    a1p, a2p = lora(a1, a2)
    g1p, g2p = lora(g1, g2)
    args = [x, x, gn, mix, w_rkv[0].astype(BF16), w_rkv[1].astype(BF16), w_rkv[2].astype(BF16),
            w0[None, :], w1p, w2p, a0[None, :], a1p, a2p, g1p, g2p, k_k[None, :], k_a[None, :], _head_sum_matrix()]
    full = lambda a: pl.BlockSpec(a.shape, const)
    in_specs = [pl.BlockSpec((tm, D), row),
                pl.BlockSpec((8, D), lambda i: (jnp.maximum(i * (tm // 8) - 1, 0), 0))]
    in_specs += [full(a) for a in args[2:]]
    if has_vres:
        v0, v1, v2 = v_res
        v1p, v2p = lora(v1, v2)
        extra = [v_first, v0[None, :], v1p, v2p]
        args += extra
        in_specs += [pl.BlockSpec((tm, D), row)] + [full(a) for a in extra[1:]]
    outs = pl.pallas_call(
        functools.partial(_rw_pre_kernel, has_vres=has_vres, seq_tiles=S // tm),
        grid=(T // tm,),
        in_specs=in_specs,
        out_specs=[pl.BlockSpec((tm, D), row)] * 7,
        out_shape=[jax.ShapeDtypeStruct((T, D), F32)] * 7,
        compiler_params=_cparams(("parallel",)),
        name="rw_pre",
    )(*args)
    return outs


def _rw_scan_kernel(r_ref, lw_ref, k_ref, v_ref, kn_ref, b_ref, y_ref, s_ref):
    L = RW_CHUNK

    @pl.when(pl.program_id(1) == 0)
    def _():
        s_ref[...] = jnp.zeros_like(s_ref)

    nb = r_ref.shape[0]

    def rows(ref):
        return jnp.concatenate([ref[s] for s in range(nb)], axis=0)

    lw = rows(lw_ref)
    tr = lax.broadcasted_iota(I32, (nb * L, nb * L), 0)
    tc = lax.broadcasted_iota(I32, (nb * L, nb * L), 1)
    tri = jnp.where((tr >= tc) & (tr // L == tc // L), 1.0, 0.0).astype(BF16)
    hi, mid, lo = _split3(lw)
    lwc = _dot(tri, hi) + _dot(tri, mid) + _dot(tri, lo)
    wl = jnp.concatenate([jnp.broadcast_to(lwc[(s + 1) * L - 1:(s + 1) * L, :], (L, lwc.shape[1]))
                          for s in range(nb)], axis=0)
    w_inc = jnp.exp(lwc)
    w_exc = jnp.exp(lwc - lw)
    w_inv = jnp.exp(-lwc)
    w_end = jnp.exp(wl - lwc)
    w_all = jnp.exp(wl)
    kv = rows(k_ref)
    bv = rows(b_ref)
    vv = rows(v_ref)
    rt = rows(r_ref) * w_inc
    at = -rows(kn_ref) * w_exc
    kt = kv * w_inv
    bt = bv * w_inv
    ke = kv * w_end
    be = bv * w_end

    lane = lax.broadcasted_iota(I32, (1, LANES), 1)
    first = lane < HEAD_DIM

    def sm(x):
        return jnp.concatenate([jnp.where(first, x, 0.0), jnp.where(first, 0.0, x)], axis=0)

    n2 = 2 * L
    ri = lax.broadcasted_iota(I32, (2 * n2, 2 * n2), 0)
    ci = lax.broadcasted_iota(I32, (2 * n2, 2 * n2), 1)
    causal = (ri % L) + ri // n2 > (ci % L)
    bi = lax.broadcasted_iota(I32, (n2, n2), 0) // L
    bj = lax.broadcasted_iota(I32, (n2, n2), 1) // L
    blockdiag = bi == bj
    eye = jnp.where(lax.broadcasted_iota(I32, (n2, n2), 0) == lax.broadcasted_iota(I32, (n2, n2), 1), 1.0, 0.0)

    ents = [(s, p) for s in range(nb) for p in range(N_PAIRS)]
    prs = range(len(ents))
    cut = [(slice(s * L, (s + 1) * L), slice(p * LANES, (p + 1) * LANES)) for s, p in ents]
    vsb = [sm(vv[c]).astype(BF16) for c in cut]
    a_l = [jnp.concatenate([sm(at[c]), sm(rt[c])], axis=0).astype(BF16) for c in cut]
    a_r = [jnp.concatenate([sm(bt[c]), sm(kt[c])], axis=0).astype(BF16) for c in cut]
    m = [jnp.where(causal, _dot_nt(a_l[e], a_r[e]), 0.0) for e in prs]
    s_old = [s_ref[e] for e in prs]
    p12 = [_dot_nt(a_l[e], s_old[e].astype(BF16)) for e in prs]
    pm = [m[e][:n2, :n2].astype(BF16) for e in prs]
    tm = [eye + m[e][:n2, :n2] for e in prs]
    for _ in range(5):
        pm = [_dot(pm[e], pm[e]).astype(BF16) for e in prs]
        tm = [tm[e] + _dot(tm[e].astype(BF16), pm[e]) for e in prs]
    z = [p12[e][:n2] + _dot(m[e][:n2, n2:].astype(BF16), vsb[e]) for e in prs]
    u = [_dot(tm[e].astype(BF16), z[e].astype(BF16)) for e in prs]
    y = [p12[e][n2:] + _dot(m[e][n2:, :].astype(BF16), jnp.concatenate([u[e].astype(BF16), vsb[e]], axis=0))
         for e in prs]
    upd = [_dot_tn(jnp.concatenate([u[e][:L] + u[e][L:], vv[cut[e]]], axis=0).astype(BF16),
                   jnp.concatenate([be[cut[e]], ke[cut[e]]], axis=0).astype(BF16)) for e in prs]
    for e, (s, p) in enumerate(ents):
        y_ref[s, :, cut[e][1]] = y[e][:L] + y[e][L:]
        s_ref[e] = s_old[e] * w_all[s * L:s * L + 1, cut[e][1]] + jnp.where(blockdiag, upd[e], 0.0)


def _rw_scan(r, lw, k, v, kn, b, B, S):
    D = D_MODEL
    L = RW_CHUNK
    nb = 2 if B % 2 == 0 else 1
    spec = pl.BlockSpec((nb, L, D), lambda bb, c: (bb, c, 0))
    args = [a.reshape(B, S, D) for a in (r, lw, k, v, kn, b)]
    y = pl.pallas_call(
        _rw_scan_kernel,
        grid=(B // nb, S // L),
        in_specs=[spec] * 6,
        out_specs=spec,
        out_shape=jax.ShapeDtypeStruct((B, S, D), F32),
        scratch_shapes=[pltpu.VMEM((nb * N_PAIRS, LANES, LANES), F32)],
        compiler_params=_cparams(("parallel", "arbitrary")),
        name="rw_scan",
    )(*args)
    return y.reshape(B * S, D)


def _rw_post_kernel(y_ref, r_ref, k_ref, v_ref, g_ref, rk_ref, lnw_ref, lnb_ref, hp_ref, o_ref):
    hp = hp_ref[...]
    y = y_ref[...]
    inv = 1.0 / HEAD_DIM
    mu = _head_sum(y, hp) * inv
    yc = y - mu
    var = _head_sum(yc * yc, hp) * inv
    yn = yc * lax.rsqrt(var + RW_GN_EPS) * lnw_ref[...] + lnb_ref[...]
    bonus = _head_sum(r_ref[...] * k_ref[...] * rk_ref[...], hp) * v_ref[...]
    o_ref[...] = ((yn + bonus) * g_ref[...]).astype(o_ref.dtype)


def _rw_post(y, r, k, v, g, r_k, ln_w, ln_b, *, tm=256):
    T, D = y.shape
    row = lambda i: (i, 0)
    const = lambda i: (0, 0)
    return pl.pallas_call(
        _rw_post_kernel,
        grid=(T // tm,),
        in_specs=[pl.BlockSpec((tm, D), row)] * 5 + [pl.BlockSpec((1, D), const)] * 3
        + [pl.BlockSpec((2 * HS_W, HS_W), const)],
        out_specs=pl.BlockSpec((tm, D), row),
        out_shape=jax.ShapeDtypeStruct((T, D), BF16),
        compiler_params=_cparams(("parallel",)),
        name="rw_post",
    )(y, r, k, v, g, r_k.reshape(1, D), ln_w[None, :], ln_b[None, :], _head_sum_matrix())


def _rwkv_mixer(x, gn, B, S, mix, w_rkv, w0, w1, w2, a0, a1, a2, g1, g2, k_k, k_a, r_k, ln_w, ln_b, v_first, v_res):
    r, lw, k, v, kn, b, g = _rw_pre(x, gn, mix, w_rkv, w0, w1, w2, a0, a1, a2, g1, g2, k_k, k_a, v_first, v_res, S)
    y = _rw_scan(r, lw, k, v, kn, b, B, S)
    return _rw_post(y, r, k, v, g, r_k, ln_w, ln_b), v


def _sb_mixer(x, gn, B, S, w_qkv):
    qkv = _norm_proj(x, gn, w_qkv.astype(BF16))
    return _sb_attention(qkv.reshape(B, S, 3 * D_MODEL), B, S).reshape(B * S, D_MODEL)


def _dsa_mixer(x, gn, B, S, w_in, q_norm, k_norm):
    qkv, qcat, kcat, wi, kmx = _dsa_proj(x, gn, w_in, q_norm, k_norm, B, S)
    o = _dsa_attention(qkv.reshape(B, S, 3 * D_MODEL), qcat, kcat, wi, kmx, B, S)
    return o.reshape(B * S, D_MODEL)


def kernel(x, norm_mix, norm_ffn, ffn_w_gu, ffn_w_down, rw_mix, rw_w_rkv, rw_w0, rw_w1, rw_w2, rw_a0, rw_a1, rw_a2, rw_g1, rw_g2, rw_v0, rw_v1, rw_v2, rw_k_k, rw_k_a, rw_r_k, rw_ln_w, rw_ln_b, rw_w_out, sb_w_qkv, sb_w_out, ds_w_in, ds_q_norm, ds_k_norm, ds_w_out):
    B, S, D = x.shape
    depth = norm_mix.shape[0]
    xf = x.reshape(B * S, D)
    v_first = None
    for i in range(depth):
        kind, j = i % 3, i // 3
        gn = norm_mix[i][None, :]
        if kind == 0:
            v_res = None if j == 0 else (rw_v0[j - 1], rw_v1[j - 1], rw_v2[j - 1])
            o, v_layer = _rwkv_mixer(xf, gn, B, S, rw_mix[j], rw_w_rkv[j], rw_w0[j], rw_w1[j], rw_w2[j],
                                     rw_a0[j], rw_a1[j], rw_a2[j], rw_g1[j], rw_g2[j], rw_k_k[j], rw_k_a[j],
                                     rw_r_k[j], rw_ln_w[j], rw_ln_b[j], v_first, v_res)
            if j == 0:
                v_first = v_layer
            w_out = rw_w_out[j]
        elif kind == 1:
            o = _sb_mixer(xf, gn, B, S, sb_w_qkv[j])
            w_out = sb_w_out[j]
        else:
            o = _dsa_mixer(xf, gn, B, S, ds_w_in[j], ds_q_norm[j], ds_k_norm[j])
            w_out = ds_w_out[j]
        xf = _ffn_layer(xf, o, w_out.astype(BF16), norm_ffn[i][None, :], ffn_w_gu[i].astype(BF16),
                        ffn_w_down[i].astype(BF16))
    return xf.reshape(B, S, D)
```

```python
import functools
import math

import jax
import jax.numpy as jnp
from jax import lax
from jax.experimental import pallas as pl
from jax.experimental.pallas import tpu as pltpu

F32 = jnp.float32
BF16 = jnp.bfloat16
I32 = jnp.int32

D_MODEL = 1024
HEAD_DIM = 64
N_HEADS = D_MODEL // HEAD_DIM
N_PAIRS = N_HEADS // 2
LANES = 128
RMS_EPS = 1e-6
ROPE_THETA = 10000.0
RW_GN_EPS = HEAD_DIM * 1e-5
RW_CHUNK = 64
DS_TOPK_MAX = 256
DS_IDX_HEADS = 8
DS_IDX_DIM = 64
DS_CHUNK = 64
NEG_BIG = -1e30
SB_DEAD_LOG = -105.0
DS_SAFE_LOGIT = 35.0
VMEM_LIMIT = 56 * 1024 * 1024


def _cparams(sem):
    return pltpu.CompilerParams(dimension_semantics=sem, vmem_limit_bytes=VMEM_LIMIT)


def _dot(a, b):
    return jnp.dot(a, b, preferred_element_type=F32)


def _dot_nt(a, b):
    return lax.dot_general(a, b, (((1,), (1,)), ((), ())), preferred_element_type=F32)


def _dot_tn(a, b):
    return lax.dot_general(a, b, (((0,), (0,)), ((), ())), preferred_element_type=F32)


def _split2(x):
    hi = x.astype(BF16)
    lo = (x - hi.astype(F32)).astype(BF16)
    return hi, lo


def _split3(x):
    hi = x.astype(BF16)
    r1 = x - hi.astype(F32)
    mid = r1.astype(BF16)
    lo = (r1 - mid.astype(F32)).astype(BF16)
    return hi, mid, lo


HS_W = 256


def _head_sum(x, hp):
    hi, lo = _split2(x)
    outs = []
    for s in range(x.shape[1] // HS_W):
        sl = slice(s * HS_W, (s + 1) * HS_W)
        outs.append(_dot(jnp.concatenate([hi[:, sl], lo[:, sl]], axis=1), hp))
    return jnp.concatenate(outs, axis=1)


def _rms(x, g):
    ms = jnp.mean(x * x, axis=-1, keepdims=True)
    return x * lax.rsqrt(ms + RMS_EPS) * g


def _ffn_kernel(x_ref, o_ref, wo_ref, g_ref, wgu_ref, wd_ref, out_ref, x1_ref, hn_ref, acc_ref, act_ref, *, tf):
    F = wd_ref.shape[0]
    nf = F // tf
    x1 = x_ref[...] + _dot(o_ref[...], wo_ref[...])
    x1_ref[...] = x1
    hn_ref[...] = _rms(x1, g_ref[...]).astype(BF16)
    for f in range(nf + 1):
        if f < nf:
            hn = hn_ref[...]
            g = _dot(hn, wgu_ref[:, f * tf:(f + 1) * tf])
            u = _dot(hn, wgu_ref[:, F + f * tf:F + (f + 1) * tf])
        if f > 0:
            d = _dot(act_ref[(f - 1) % 2], wd_ref[(f - 1) * tf:f * tf, :])
            if f == 1:
                acc_ref[...] = d
            else:
                acc_ref[...] += d
        if f < nf:
            act_ref[f % 2] = (g * jax.nn.sigmoid(g) * u).astype(BF16)
    out_ref[...] = x1_ref[...] + acc_ref[...]


def _ffn_layer(x, o, w_out, g_ffn, w_gu, w_down, *, tm=512, tf=256):
    T, D = x.shape
    F = w_down.shape[0]
    assert T % tm == 0 and F % tf == 0
    row = lambda i: (i, 0)
    const = lambda i: (0, 0)
    resident = lambda a: pl.BlockSpec(a.shape, const, pipeline_mode=pl.Buffered(1))
    return pl.pallas_call(
        functools.partial(_ffn_kernel, tf=tf),
        grid=(T // tm,),
        in_specs=[
            pl.BlockSpec((tm, D), row),
            pl.BlockSpec((tm, D), row),
            resident(w_out),
            pl.BlockSpec((1, D), const),
            resident(w_gu),
            resident(w_down),
        ],
        out_specs=pl.BlockSpec((tm, D), row),
        out_shape=jax.ShapeDtypeStruct((T, D), F32),
        scratch_shapes=[pltpu.VMEM((tm, D), F32), pltpu.VMEM((tm, D), BF16), pltpu.VMEM((tm, D), F32),
                        pltpu.VMEM((2, tm, tf), BF16)],
        compiler_params=_cparams(("parallel",)),
        name="ffn",
    )(x, o, w_out, g_ffn, w_gu, w_down)


def _norm_proj_kernel(x_ref, g_ref, w_ref, out_ref, hn_ref):
    @pl.when(pl.program_id(1) == 0)
    def _():
        hn_ref[...] = _rms(x_ref[...], g_ref[...]).astype(BF16)

    out_ref[...] = _dot(hn_ref[...], w_ref[...]).astype(out_ref.dtype)


def _norm_proj(x, g, w, *, tm=512, tn=1024):
    T, D = x.shape
    N = w.shape[1]
    return pl.pallas_call(
        _norm_proj_kernel,
        grid=(T // tm, N // tn),
        in_specs=[
            pl.BlockSpec((tm, D), lambda i, j: (i, 0)),
            pl.BlockSpec((1, D), lambda i, j: (0, 0)),
            pl.BlockSpec((D, tn), lambda i, j: (0, j)),
        ],
        out_specs=pl.BlockSpec((tm, tn), lambda i, j: (i, j)),
        out_shape=jax.ShapeDtypeStruct((T, N), BF16),
        scratch_shapes=[pltpu.VMEM((tm, D), BF16)],
        compiler_params=_cparams(("parallel", "arbitrary")),
        name="norm_proj",
    )(x, g, w)


def _sb_kernel(q_ref, k_ref, v_ref, suo_ref, o_ref, c_ref, a_ref, zs_ref, hl_ref, *, tq, tk):
    qi = pl.program_id(2)
    lane = lax.broadcasted_iota(I32, (1, LANES), 1)
    first = lane < HEAD_DIM
    q = q_ref[...] * jnp.asarray(HEAD_DIM ** -0.5, BF16)
    zero = jnp.zeros_like(q)
    qh = (jnp.where(first, q, zero), jnp.where(first, zero, q))
    suo = suo_ref[...]
    c_ref[...] = jnp.zeros_like(c_ref)
    a_ref[...] = jnp.zeros_like(a_ref)
    qpos = qi * tq + lax.broadcasted_iota(I32, (tq, 1), 0)
    def tile(ks, w, masked):
        nsub = w // tk
        ks = pl.multiple_of(ks, tq)
        kb = k_ref[pl.ds(ks, w), :]
        zz = [_dot_nt(qh[h], kb) for h in range(2)]
        if masked:
            earlier = (ks + lax.broadcasted_iota(I32, (1, w), 1)) < qpos
        for h in range(2):
            z = zz[h]
            sp = jnp.maximum(z, 0.0) + jnp.log(1.0 + jnp.exp(-jnp.abs(z)))
            zs = z - sp
            if masked:
                sp = jnp.where(earlier, sp, 0.0)
                zs = jnp.where(earlier, zs, NEG_BIG)
            zs_ref[h, :, :w] = zs
            for u in range(nsub):
                hi, lo = _split2(sp[:, u * tk:(u + 1) * tk])
                hl_ref[h, :, 2 * u * tk:(2 * u + 1) * tk] = hi
                hl_ref[h, :, (2 * u + 1) * tk:(2 * u + 2) * tk] = lo
        cs = [[_dot(hl_ref[h, :, 2 * u * tk:(2 * u + 2) * tk], suo) for u in range(nsub)] for h in range(2)]
        vb = v_ref[pl.ds(ks, w), :]
        zv = jnp.zeros_like(vb)
        vb2 = jnp.concatenate([jnp.where(first, vb, zv), jnp.where(first, zv, vb)], axis=0)
        att = []
        for h in range(2):
            c = c_ref[h]
            parts = [None] * nsub
            for u in reversed(range(nsub)):
                a = jnp.exp(zs_ref[h, :, u * tk:(u + 1) * tk] + cs[h][u][:, :tk] + c)
                parts[u] = a.astype(BF16)
                c = c + cs[h][u][:, tk:]
            c_ref[h] = c
            att += parts
        a_ref[...] += _dot(jnp.concatenate(att, axis=1), vb2)

    def c_max():
        return jnp.max(jnp.maximum(c_ref[0], c_ref[1]))

    s0 = jnp.maximum(qi - 1, 0)
    tile(s0 * tq, 2 * tq, True)

    def keep_going(carry):
        t, cm = carry
        return jnp.logical_and(t < s0, cm > SB_DEAD_LOG)

    def step(carry):
        t, _ = carry
        tile((s0 - 1 - t) * tq, tq, False)
        return t + 1, c_max()

    lax.while_loop(keep_going, step, (jnp.int32(0), c_max()))

    o_ref[...] = a_ref[...].astype(o_ref.dtype)


def _sb_attention(qkv, B, S, *, tq=256, tk=128):
    D = D_MODEL
    r = lax.broadcasted_iota(I32, (tk, 2 * tk), 0)
    c = lax.broadcasted_iota(I32, (tk, 2 * tk), 1)
    suo = jnp.where((c >= tk) | (r > c), -1.0, 0.0).astype(BF16)
    suo = jnp.concatenate([suo, suo], axis=0)
    kern = functools.partial(_sb_kernel, tq=tq, tk=tk)
    return pl.pallas_call(
        kern,
        grid=(B, N_PAIRS, S // tq),
        in_specs=[
            pl.BlockSpec((None, tq, LANES), lambda b, p, i: (b, i, p)),
            pl.BlockSpec((None, S, LANES), lambda b, p, i: (b, 0, N_PAIRS + p)),
            pl.BlockSpec((None, S, LANES), lambda b, p, i: (b, 0, 2 * N_PAIRS + p)),
            pl.BlockSpec((2 * tk, 2 * tk), lambda b, p, i: (0, 0)),
        ],
        out_specs=pl.BlockSpec((None, tq, LANES), lambda b, p, i: (b, i, p)),
        out_shape=jax.ShapeDtypeStruct((B, S, D), BF16),
        scratch_shapes=[pltpu.VMEM((2, tq, tk), F32), pltpu.VMEM((tq, LANES), F32),
                        pltpu.VMEM((2, tq, 2 * tq), F32), pltpu.VMEM((2, tq, 4 * tq), BF16)],
        compiler_params=_cparams(("parallel", "parallel", "arbitrary")),
        name="sb_attn",
    )(qkv, qkv, qkv, suo)


def _rope_slabs(y, cos, sin):
    lane = lax.broadcasted_iota(I32, (1, LANES), 1)
    first = (lane % HEAD_DIM) < (HEAD_DIM // 2)
    outs = []
    for s in range(y.shape[1] // LANES):
        ys = y[:, s * LANES:(s + 1) * LANES]
        rot = jnp.where(first, pltpu.roll(ys, LANES - HEAD_DIM // 2, axis=1), pltpu.roll(ys, HEAD_DIM // 2, axis=1))
        outs.append(ys * cos[:, s * LANES:(s + 1) * LANES] + rot * sin[:, s * LANES:(s + 1) * LANES])
    return jnp.concatenate(outs, axis=1)


def _dsa_proj_kernel(x_ref, g_ref, w_ref, wih_ref, wil_ref, hp_ref, qn_ref, kn_ref, cos_ref, sin_ref,
                     qkv_ref, qcat_ref, kcat_ref, wi_ref, kmx_ref, hh_ref, hl_ref, *, wi_scale):
    j = pl.program_id(1)
    nq = DS_IDX_HEADS * DS_IDX_DIM

    @pl.when(j == 0)
    def _():
        hn = _rms(x_ref[...], g_ref[...])
        hh, hl = _split2(hn)
        hh_ref[...] = hh
        hl_ref[...] = hl
        wih = wih_ref[...]
        idx = _dot(hh, wih) + _dot(hl, wih) + _dot(hh, wil_ref[...])
        cos = cos_ref[...]
        sin = sin_ref[...]
        lane = lax.broadcasted_iota(I32, (1, LANES), 1)
        first = lane < DS_IDX_DIM

        def hi_lo(x):
            hi = x.astype(BF16).astype(F32)
            return hi, x - hi

        qi = _rope_slabs(idx[:, :nq], cos[:, :nq], sin[:, :nq])
        qhi, qlo = hi_lo(qi)
        for h in range(DS_IDX_HEADS):
            sl = slice((h // 2) * LANES, (h // 2 + 1) * LANES)
            if h % 2 == 0:
                own_hi = qhi[:, sl]
                a = jnp.where(first, own_hi, pltpu.roll(qlo[:, sl], DS_IDX_DIM, axis=1))
            else:
                own_hi = pltpu.roll(qhi[:, sl], DS_IDX_DIM, axis=1)
                a = jnp.where(first, own_hi, qlo[:, sl])
            b = jnp.where(first, own_hi, 0.0)
            qcat_ref[:, 2 * h * LANES:(2 * h + 1) * LANES] = a.astype(BF16)
            qcat_ref[:, (2 * h + 1) * LANES:(2 * h + 2) * LANES] = b.astype(BF16)
        tail = idx[:, nq:nq + LANES]
        ki = jnp.where(first, _rope_slabs(tail, cos[:, :LANES], sin[:, :LANES]), 0.0)
        khi, klo = hi_lo(ki)
        kcat_ref[:, :LANES] = (khi + pltpu.roll(khi, DS_IDX_DIM, axis=1)).astype(BF16)
        kcat_ref[:, LANES:] = klo.astype(BF16)
        wi_ref[...] = tail * wi_scale

    y = _dot(hh_ref[...], w_ref[...])

    def normed(gn_ref, scale):
        ms = _head_sum(y * y, hp_ref[...]) * (1.0 / HEAD_DIM)
        yn = y * lax.rsqrt(ms + RMS_EPS) * gn_ref[...]
        out = _rope_slabs(yn, cos_ref[...], sin_ref[...])
        return (out * scale).astype(qkv_ref.dtype)

    @pl.when(j == 0)
    def _():
        qkv_ref[...] = normed(qn_ref, HEAD_DIM ** -0.5)

    @pl.when(j == 1)
    def _():
        kb = normed(kn_ref, 1.0)
        qkv_ref[...] = kb
        kf = kb.astype(F32)
        kmx_ref[...] = jnp.max(_head_sum(kf * kf, hp_ref[...]), axis=0, keepdims=True)

    @pl.when(j == 2)
    def _():
        qkv_ref[...] = y.astype(qkv_ref.dtype)


def _head_sum_matrix():
    r = (lax.broadcasted_iota(I32, (2 * HS_W, HS_W), 0) % HS_W) // HEAD_DIM
    c = lax.broadcasted_iota(I32, (2 * HS_W, HS_W), 1) // HEAD_DIM
    return jnp.where(r == c, 1.0, 0.0).astype(BF16)


def _rope_tables(S):
    half = HEAD_DIM // 2
    inv = 1.0 / (ROPE_THETA ** (jnp.arange(half, dtype=F32) / half))
    ang = jnp.arange(S, dtype=F32)[:, None] * inv[None, :]
    cos = jnp.concatenate([jnp.cos(ang), jnp.cos(ang)], axis=1)
    sin = jnp.concatenate([-jnp.sin(ang), jnp.sin(ang)], axis=1)
    return jnp.tile(cos, (1, N_HEADS)), jnp.tile(sin, (1, N_HEADS))


def _dsa_proj(x, g, w_in, q_norm, k_norm, B, S, *, tm=256):
    T, D = x.shape
    c3 = 3 * D
    nq = DS_IDX_HEADS * DS_IDX_DIM
    n_idx = nq + DS_IDX_DIM + DS_IDX_HEADS
    w_qkv = w_in[:, :c3].astype(BF16)
    w_idx = jnp.pad(w_in[:, c3:], ((0, 0), (0, nq + LANES - n_idx)))
    wih = w_idx.astype(BF16)
    wil = (w_idx - wih.astype(F32)).astype(BF16)
    cos, sin = _rope_tables(S)
    qn = jnp.tile(q_norm, N_HEADS)[None, :]
    kn = jnp.tile(k_norm, N_HEADS)[None, :]
    wi_scale = DS_IDX_HEADS ** -0.5 * DS_IDX_DIM ** -0.5
    nsb = S // tm
    row = lambda i, j: (i, 0)
    const = lambda i, j: (0, 0)
    pos = lambda i, j: (i % nsb, 0)
    return pl.pallas_call(
        functools.partial(_dsa_proj_kernel, wi_scale=wi_scale),
        grid=(T // tm, 3),
        in_specs=[
            pl.BlockSpec((tm, D), row),
            pl.BlockSpec((1, D), const),
            pl.BlockSpec((D, D), lambda i, j: (0, j)),
            pl.BlockSpec((D, nq + LANES), const),
            pl.BlockSpec((D, nq + LANES), const),
            pl.BlockSpec((2 * HS_W, HS_W), const),
            pl.BlockSpec((1, D), const),
            pl.BlockSpec((1, D), const),
            pl.BlockSpec((tm, D), pos),
            pl.BlockSpec((tm, D), pos),
        ],
        out_specs=[
            pl.BlockSpec((tm, D), lambda i, j: (i, j)),
            pl.BlockSpec((tm, 2 * LANES * DS_IDX_HEADS), row),
            pl.BlockSpec((tm, 2 * LANES), row),
            pl.BlockSpec((tm, LANES), row),
            pl.BlockSpec((None, 1, D), lambda i, j: (i, 0, 0)),
        ],
        out_shape=[
            jax.ShapeDtypeStruct((T, c3), BF16),
            jax.ShapeDtypeStruct((T, 2 * LANES * DS_IDX_HEADS), BF16),
            jax.ShapeDtypeStruct((T, 2 * LANES), BF16),
            jax.ShapeDtypeStruct((T, LANES), F32),
            jax.ShapeDtypeStruct((T // tm, 1, D), F32),
        ],
        scratch_shapes=[pltpu.VMEM((tm, D), BF16), pltpu.VMEM((tm, D), BF16)],
        compiler_params=_cparams(("parallel", "arbitrary")),
        name="dsa_proj",
    )(x, g, w_qkv, wih, wil, _head_sum_matrix(), qn, kn, cos, sin)


def _dsa_attn_kernel(q_ref, k_ref, v_ref, qcat_ref, kcat_ref, wi_ref, kmx_ref, tri_ref, o_ref,
                     keys_ref, bias_ref, thr_ref, need_ref, eqb_ref, m_ref, mx_ref, acc_ref,
                     *, tq, ta, rb, n_sel):
    i = pl.program_id(1)
    p = pl.program_id(2)
    lane = lax.broadcasted_iota(I32, (1, LANES), 1)
    row = lax.broadcasted_iota(I32, (tq, 1), 0)
    vis_row = i * tq + (row // DS_CHUNK + 1) * DS_CHUNK
    vis_tile = (i + 1) * tq
    n_a = (vis_tile + ta - 1) // ta

    @pl.when(p == 0)
    def _select():
        wi = wi_ref[...]

        def score_blk(c, _):
            cs = pl.multiple_of(c * ta, ta)
            kc = kcat_ref[pl.ds(cs, ta), :]
            idxs = [_dot_nt(qcat_ref[:, 2 * h * LANES:(2 * h + 2) * LANES], kc) for h in range(DS_IDX_HEADS)]
            sc = jnp.zeros((tq, ta), F32)
            for h in range(DS_IDX_HEADS):
                sc = sc + wi[:, DS_IDX_DIM + h:DS_IDX_DIM + h + 1] * jnp.maximum(idxs[h], 0.0)
            sc = jnp.where(sc == 0.0, 0.0, sc)
            kpos = cs + lax.broadcasted_iota(I32, (1, ta), 1)
            sc = jnp.where(kpos < vis_row, sc, -jnp.inf)
            bits = lax.bitcast_convert_type(sc, I32)
            keys_ref[:, pl.ds(cs, ta)] = jnp.where(bits < 0, bits ^ jnp.int32(0x7FFFFFFF), bits)
            return 0

        int_min = jnp.iinfo(jnp.int32).min
        lax.fori_loop(0, n_a, score_blk, 0)

        @pl.when(n_a % 2 == 1)
        def _():
            keys_ref[:, pl.ds(pl.multiple_of(n_a * ta, ta), ta)] = jnp.full((tq, ta), int_min, I32)

        nsweep = 2 * ta // LANES
        groups = [slice(g * rb, (g + 1) * rb) for g in range(tq // rb)]

        def count(cmps, strict):
            outs = []
            for g0 in range(0, len(groups), 2):
                def blk(c, parts, g0=g0):
                    cs = pl.multiple_of(c * 2 * ta, 2 * ta)
                    parts = list(parts)
                    for s in range(nsweep):
                        for d in range(2):
                            kk = keys_ref[groups[g0 + d], pl.ds(cs + s * LANES, LANES)]
                            hit = (kk > cmps[g0 + d]) if strict else (kk >= cmps[g0 + d])
                            parts[d] = parts[d] + jnp.where(hit, 1, 0).astype(I32)
                    return tuple(parts)
                zero = jnp.zeros((rb, LANES), I32)
                outs += list(lax.fori_loop(0, (n_a + 1) // 2, blk, (zero, zero)))
            return [jnp.sum(pt, axis=1, keepdims=True) for pt in outs]

        thr_ref[...] = jnp.full(thr_ref.shape, int_min, I32)

        def bit_step(t, _):
            bit = lax.shift_left(jnp.int32(1), 31 - t)
            cands = [thr_ref[g, :] + bit for g in groups]
            cnts = count(cands, False)
            for g, cand, cnt in zip(groups, cands, cnts):
                thr_ref[g, :] = jnp.where(cnt >= n_sel, cand, thr_ref[g, :])
            return 0

        lax.fori_loop(0, 32, bit_step, 0)
        nrep = ta // LANES
        n_ge = count([thr_ref[g, :] for g in groups], False)
        tied = sum(jnp.max(jnp.abs(cnt - n_sel)) for cnt in n_ge) > 0

        @pl.when(jnp.logical_not(tied))
        def _():
            def bias_blk(c, _):
                cs = pl.multiple_of(c * ta, ta)
                kk = keys_ref[:, pl.ds(cs, ta)]
                thr = jnp.concatenate([thr_ref[...]] * nrep, axis=1)
                kpos = cs + lax.broadcasted_iota(I32, (1, ta), 1)
                bias_ref[:, pl.ds(cs, ta)] = jnp.where(kpos < vis_row, jnp.where(kk >= thr, 0.0, NEG_BIG), NEG_BIG)
                return 0

            lax.fori_loop(0, n_a, bias_blk, 0)

        @pl.when(tied)
        def _():
            for g, cnt in zip(groups, count([thr_ref[g, :] for g in groups], True)):
                need_ref[g, :] = jnp.broadcast_to((n_sel - cnt).astype(F32), (rb, LANES))
            tri = tri_ref[...]

            def bias_blk(c, _):
                cs = pl.multiple_of(c * ta, ta)
                kk = keys_ref[:, pl.ds(cs, ta)]
                thr = jnp.concatenate([thr_ref[...]] * nrep, axis=1)
                eqf = jnp.where(kk == thr, 1.0, 0.0)
                ranks = _dot(eqf.astype(BF16), tri)
                rank = jnp.concatenate([eqb_ref[...]] * nrep, axis=1) + ranks[:, :ta]
                need = jnp.concatenate([need_ref[...]] * nrep, axis=1)
                sel = jnp.where(kk > thr, 1.0, jnp.where(rank < need, eqf, 0.0))
                kpos = cs + lax.broadcasted_iota(I32, (1, ta), 1)
                sel = jnp.where(kpos < vis_row, sel, 0.0)
                bias_ref[:, pl.ds(cs, ta)] = (sel - 1.0) * (-NEG_BIG)
                eqb_ref[...] += ranks[:, ta:]
                return 0

            eqb_ref[...] = jnp.zeros_like(eqb_ref)
            lax.fori_loop(0, n_a, bias_blk, 0)

        @pl.when(n_a % 2 == 1)
        def _():
            bias_ref[:, pl.ds(pl.multiple_of(n_a * ta, ta), ta)] = jnp.full((tq, ta), NEG_BIG, F32)

    q = q_ref[...]
    zero = jnp.zeros_like(q)
    first = lane < HEAD_DIM
    qh2 = (jnp.where(first, q, zero), jnp.where(first, zero, q))
    n_pair = (n_a + 1) // 2

    trow = lax.broadcasted_iota(I32, (kmx_ref.shape[0], 1), 0)
    kmax2 = jnp.max(jnp.where(trow <= i, kmx_ref[...], 0.0), axis=0, keepdims=True)
    hr = lax.broadcasted_iota(I32, (LANES, LANES), 0) // HEAD_DIM
    hc = lax.broadcasted_iota(I32, (LANES, LANES), 1) // HEAD_DIM
    qf = q.astype(F32)
    q2 = _dot((qf * qf).astype(BF16), jnp.where(hr == hc, 1.0, 0.0).astype(BF16))
    bound2 = 1.05 * jnp.max(q2 * kmax2)
    m_ref[...] = jnp.zeros_like(m_ref)

    @pl.when(bound2 > DS_SAFE_LOGIT * DS_SAFE_LOGIT)
    def _row_max():
        mx_ref[...] = jnp.full(mx_ref.shape, NEG_BIG, F32)

        def max_blk(c2, _):
            for u in range(2):
                cs = pl.multiple_of((2 * c2 + u) * ta, ta)
                kb = k_ref[pl.ds(cs, ta), :]
                bias = bias_ref[:, pl.ds(cs, ta)]
                for h in range(2):
                    mx_ref[u, h] = jnp.maximum(mx_ref[u, h], _dot_nt(qh2[h], kb) + bias)
            return 0

        lax.fori_loop(0, n_pair, max_blk, 0)
        for h in range(2):
            m_ref[h] = jnp.broadcast_to(jnp.max(jnp.maximum(mx_ref[0, h], mx_ref[1, h]), axis=1, keepdims=True),
                                        (tq, LANES))

    ones_h = [jnp.broadcast_to(jnp.where(first, a, b), (ta, LANES)).astype(BF16) for a, b in ((1.0, 0.0), (0.0, 1.0))]
    acc_ref[...] = jnp.zeros_like(acc_ref)

    def attn_blocks(c0, nblk):
        css = [pl.multiple_of((c0 + u) * ta, ta) for u in range(nblk)]
        logits = [[_dot_nt(qh2[h], k_ref[pl.ds(cs, ta), :]) for h in range(2)] for cs in css]
        out = acc_ref[...]
        for u, cs in enumerate(css):
            vb = v_ref[pl.ds(cs, ta), :]
            zv = jnp.zeros_like(vb)
            vs = [jnp.concatenate([jnp.where(first, vb, zv), ones_h[0]], axis=1),
                  jnp.concatenate([jnp.where(first, zv, vb), ones_h[1]], axis=1)]
            bias = bias_ref[:, pl.ds(cs, ta)]
            for h in range(2):
                pr = jnp.exp(logits[u][h] + bias - m_ref[h][:, :1]).astype(BF16)
                out = out + _dot(pr, vs[h])
        acc_ref[...] = out

    n_quad = n_a // 4

    def quad(c4, _):
        attn_blocks(4 * c4, 4)
        return 0

    lax.fori_loop(0, n_quad, quad, 0)
    rest = n_a - 4 * n_quad

    @pl.when(rest > 0)
    def _():
        attn_blocks(4 * n_quad, 2)

    @pl.when(rest == 3)
    def _():
        attn_blocks(4 * n_quad + 2, 2)
    acc = acc_ref[...]
    o_ref[...] = (acc[:, :LANES] / acc[:, LANES:]).astype(o_ref.dtype)


def _dsa_attention(qkv, qcat, kcat, wi, kmx, B, S, *, tq=256, ta=512, rb=64):
    D = D_MODEL
    n_sel = min(DS_TOPK_MAX, S // 4)
    assert S % (2 * ta) == 0 and S // LANES <= 256 and kmx.shape[0] * tq == B * S
    nqc = 2 * LANES * DS_IDX_HEADS
    r = lax.broadcasted_iota(I32, (ta, ta + LANES), 0)
    c = lax.broadcasted_iota(I32, (ta, ta + LANES), 1)
    tri = jnp.where((r < c) | (c >= ta), 1.0, 0.0).astype(BF16)
    kern = functools.partial(_dsa_attn_kernel, tq=tq, ta=ta, rb=rb, n_sel=n_sel)
    s_pad = S
    return pl.pallas_call(
        kern,
        grid=(B, S // tq, N_PAIRS),
        in_specs=[
            pl.BlockSpec((None, tq, LANES), lambda b, i, p: (b, i, p)),
            pl.BlockSpec((None, S, LANES), lambda b, i, p: (b, 0, N_PAIRS + p)),
            pl.BlockSpec((None, S, LANES), lambda b, i, p: (b, 0, 2 * N_PAIRS + p)),
            pl.BlockSpec((None, tq, nqc), lambda b, i, p: (b, i, 0)),
            pl.BlockSpec((None, S, 2 * LANES), lambda b, i, p: (b, 0, 0)),
            pl.BlockSpec((None, tq, LANES), lambda b, i, p: (b, i, 0)),
            pl.BlockSpec((None, S // tq, LANES), lambda b, i, p: (b, 0, p)),
            pl.BlockSpec((ta, ta + LANES), lambda b, i, p: (0, 0)),
        ],
        out_specs=pl.BlockSpec((None, tq, LANES), lambda b, i, p: (b, i, p)),
        out_shape=jax.ShapeDtypeStruct((B, S, D), BF16),
        scratch_shapes=[
            pltpu.VMEM((tq, s_pad), I32),
            pltpu.VMEM((tq, s_pad), F32),
            pltpu.VMEM((tq, LANES), I32),
            pltpu.VMEM((tq, LANES), F32),
            pltpu.VMEM((tq, LANES), F32),
            pltpu.VMEM((2, tq, LANES), F32),
            pltpu.VMEM((2, 2, tq, ta), F32),
            pltpu.VMEM((tq, 2 * LANES), F32),
        ],
        compiler_params=_cparams(("parallel", "arbitrary", "arbitrary")),
        name="dsa_attn",
    )(qkv, qkv, qkv, qcat.reshape(B, S, nqc), kcat.reshape(B, S, 2 * LANES), wi.reshape(B, S, LANES),
      kmx.reshape(B, S // tq, D), tri)


def _rw_pre_kernel(*refs, has_vres, seq_tiles):
    if has_vres:
        (x_ref, xp_ref, gn_ref, mix_ref, wr_ref, wk_ref, wv_ref, w0_ref, w1_ref, w2_ref, a0_ref, a1_ref, a2_ref,
         g1_ref, g2_ref, kk_ref, ka_ref, hp_ref, vf_ref, v0_ref, v1_ref, v2_ref,
         r_out, lw_out, k_out, v_out, kn_out, b_out, g_out) = refs
    else:
        (x_ref, xp_ref, gn_ref, mix_ref, wr_ref, wk_ref, wv_ref, w0_ref, w1_ref, w2_ref, a0_ref, a1_ref, a2_ref,
         g1_ref, g2_ref, kk_ref, ka_ref, hp_ref,
         r_out, lw_out, k_out, v_out, kn_out, b_out, g_out) = refs
    i = pl.program_id(0)
    gn = gn_ref[...]
    h = _rms(x_ref[...], gn)
    tm = h.shape[0]
    hprev = _rms(xp_ref[...], gn)[7:8, :]
    hprev = jnp.where(i % seq_tiles == 0, 0.0, hprev)
    rowi = lax.broadcasted_iota(I32, (tm, 1), 0)
    sh = jnp.where(rowi == 0, hprev, pltpu.roll(h, 1, axis=0))
    dlt = sh - h
    mix = mix_ref[...]

    def stream(j):
        return (h + dlt * mix[j:j + 1, :]).astype(BF16)

    r = _dot(stream(0), wr_ref[...])
    k = _dot(stream(1), wk_ref[...])
    xv = stream(2)
    v = _dot(xv, wv_ref[...])
    w = w0_ref[...] + _dot(jnp.tanh(_dot(stream(3), w1_ref[...])).astype(BF16), w2_ref[...])
    a = jax.nn.sigmoid(a0_ref[...] + _dot(_dot(stream(4), a1_ref[...]).astype(BF16), a2_ref[...]))
    g = _dot(jax.nn.sigmoid(_dot(stream(5), g1_ref[...])).astype(BF16), g2_ref[...])
    if has_vres:
        gate = jax.nn.sigmoid(v0_ref[...] + _dot(_dot(xv, v1_ref[...]).astype(BF16), v2_ref[...]))
        v = v + (vf_ref[...] - v) * gate
    logw = -(jnp.maximum(-w, 0.0) + jnp.log1p(jnp.exp(-jnp.abs(w)))) - 0.5
    lw_out[...] = -jnp.exp(logw)
    kk = k * kk_ref[...]
    ss = _head_sum(kk * kk, hp_ref[...])
    kk = kk * lax.rsqrt(jnp.maximum(ss, 1e-24))
    r_out[...] = r
    k_out[...] = k * (1.0 + (a - 1.0) * ka_ref[...])
    v_out[...] = v
    kn_out[...] = kk
    b_out[...] = kk * a
    g_out[...] = g


def _pad_cols(w, n):
    return jnp.pad(w, ((0, 0), (0, n - w.shape[1])))


def _pad_rows(w, n):
    return jnp.pad(w, ((0, n - w.shape[0]), (0, 0)))


def _rw_pre(x, gn, mix, w_rkv, w0, w1, w2, a0, a1, a2, g1, g2, k_k, k_a, v_first, v_res, S, *, tm=256):
    T, D = x.shape
    has_vres = v_res is not None
    row = lambda i: (i, 0)
    const = lambda i: (0, 0)

    def lora(wa, wb):
        n = -(-wa.shape[1] // LANES) * LANES
        return _pad_cols(wa, n).astype(BF16), _pad_rows(wb, n).astype(BF16)

    w1p, w2p = lora(w1, w2)
    a1p, a2p = lora(a1, a2)
    g1p, g2p = lora(g1, g2)
    args = [x, x, gn, mix, w_rkv[0].astype(BF16), w_rkv[1].astype(BF16), w_rkv[2].astype(BF16),
            w0[None, :], w1p, w2p, a0[None, :], a1p, a2p, g1p, g2p, k_k[None, :], k_a[None, :], _head_sum_matrix()]
    full = lambda a: pl.BlockSpec(a.shape, const)
    in_specs = [pl.BlockSpec((tm, D), row),
                pl.BlockSpec((8, D), lambda i: (jnp.maximum(i * (tm // 8) - 1, 0), 0))]
    in_specs += [full(a) for a in args[2:]]
    if has_vres:
        v0, v1, v2 = v_res
        v1p, v2p = lora(v1, v2)
        extra = [v_first, v0[None, :], v1p, v2p]
        args += extra
        in_specs += [pl.BlockSpec((tm, D), row)] + [full(a) for a in extra[1:]]
    outs = pl.pallas_call(
        functools.partial(_rw_pre_kernel, has_vres=has_vres, seq_tiles=S // tm),
        grid=(T // tm,),
        in_specs=in_specs,
        out_specs=[pl.BlockSpec((tm, D), row)] * 7,
        out_shape=[jax.ShapeDtypeStruct((T, D), F32)] * 7,
        compiler_params=_cparams(("parallel",)),
        name="rw_pre",
    )(*args)
    return outs


def _rw_scan_kernel(r_ref, lw_ref, k_ref, v_ref, kn_ref, b_ref, y_ref, s_ref):
    L = RW_CHUNK

    @pl.when(pl.program_id(1) == 0)
    def _():
        s_ref[...] = jnp.zeros_like(s_ref)

    nb = r_ref.shape[0]

    def rows(ref):
        return jnp.concatenate([ref[s] for s in range(nb)], axis=0)

    lw = rows(lw_ref)
    tr = lax.broadcasted_iota(I32, (nb * L, nb * L), 0)
    tc = lax.broadcasted_iota(I32, (nb * L, nb * L), 1)
    tri = jnp.where((tr >= tc) & (tr // L == tc // L), 1.0, 0.0).astype(BF16)
    hi, mid, lo = _split3(lw)
    lwc = _dot(tri, hi) + _dot(tri, mid) + _dot(tri, lo)
    wl = jnp.concatenate([jnp.broadcast_to(lwc[(s + 1) * L - 1:(s + 1) * L, :], (L, lwc.shape[1]))
                          for s in range(nb)], axis=0)
    w_inc = jnp.exp(lwc)
    w_exc = jnp.exp(lwc - lw)
    w_inv = jnp.exp(-lwc)
    w_end = jnp.exp(wl - lwc)
    w_all = jnp.exp(wl)
    kv = rows(k_ref)
    bv = rows(b_ref)
    vv = rows(v_ref)
    rt = rows(r_ref) * w_inc
    at = -rows(kn_ref) * w_exc
    kt = kv * w_inv
    bt = bv * w_inv
    ke = kv * w_end
    be = bv * w_end

    lane = lax.broadcasted_iota(I32, (1, LANES), 1)
    first = lane < HEAD_DIM

    def sm(x):
        return jnp.concatenate([jnp.where(first, x, 0.0), jnp.where(first, 0.0, x)], axis=0)

    n2 = 2 * L
    ri = lax.broadcasted_iota(I32, (2 * n2, 2 * n2), 0)
    ci = lax.broadcasted_iota(I32, (2 * n2, 2 * n2), 1)
    causal = (ri % L) + ri // n2 > (ci % L)
    bi = lax.broadcasted_iota(I32, (n2, n2), 0) // L
    bj = lax.broadcasted_iota(I32, (n2, n2), 1) // L
    blockdiag = bi == bj
    eye = jnp.where(lax.broadcasted_iota(I32, (n2, n2), 0) == lax.broadcasted_iota(I32, (n2, n2), 1), 1.0, 0.0)

    ents = [(s, p) for s in range(nb) for p in range(N_PAIRS)]
    prs = range(len(ents))
    cut = [(slice(s * L, (s + 1) * L), slice(p * LANES, (p + 1) * LANES)) for s, p in ents]
    vsb = [sm(vv[c]).astype(BF16) for c in cut]
    a_l = [jnp.concatenate([sm(at[c]), sm(rt[c])], axis=0).astype(BF16) for c in cut]
    a_r = [jnp.concatenate([sm(bt[c]), sm(kt[c])], axis=0).astype(BF16) for c in cut]
    m = [jnp.where(causal, _dot_nt(a_l[e], a_r[e]), 0.0) for e in prs]
    s_old = [s_ref[e] for e in prs]
    p12 = [_dot_nt(a_l[e], s_old[e].astype(BF16)) for e in prs]
    pm = [m[e][:n2, :n2].astype(BF16) for e in prs]
    tm = [eye + m[e][:n2, :n2] for e in prs]
    for _ in range(5):
        pm = [_dot(pm[e], pm[e]).astype(BF16) for e in prs]
        tm = [tm[e] + _dot(tm[e].astype(BF16), pm[e]) for e in prs]
    z = [p12[e][:n2] + _dot(m[e][:n2, n2:].astype(BF16), vsb[e]) for e in prs]
    u = [_dot(tm[e].astype(BF16), z[e].astype(BF16)) for e in prs]
    y = [p12[e][n2:] + _dot(m[e][n2:, :].astype(BF16), jnp.concatenate([u[e].astype(BF16), vsb[e]], axis=0))
         for e in prs]
    upd = [_dot_tn(jnp.concatenate([u[e][:L] + u[e][L:], vv[cut[e]]], axis=0).astype(BF16),
                   jnp.concatenate([be[cut[e]], ke[cut[e]]], axis=0).astype(BF16)) for e in prs]
    for e, (s, p) in enumerate(ents):
        y_ref[s, :, cut[e][1]] = y[e][:L] + y[e][L:]
        s_ref[e] = s_old[e] * w_all[s * L:s * L + 1, cut[e][1]] + jnp.where(blockdiag, upd[e], 0.0)


def _rw_scan(r, lw, k, v, kn, b, B, S):
    D = D_MODEL
    L = RW_CHUNK
    nb = 2 if B % 2 == 0 else 1
    spec = pl.BlockSpec((nb, L, D), lambda bb, c: (bb, c, 0))
    args = [a.reshape(B, S, D) for a in (r, lw, k, v, kn, b)]
    y = pl.pallas_call(
        _rw_scan_kernel,
        grid=(B // nb, S // L),
        in_specs=[spec] * 6,
        out_specs=spec,
        out_shape=jax.ShapeDtypeStruct((B, S, D), F32),
        scratch_shapes=[pltpu.VMEM((nb * N_PAIRS, LANES, LANES), F32)],
        compiler_params=_cparams(("parallel", "arbitrary")),
        name="rw_scan",
    )(*args)
    return y.reshape(B * S, D)


def _rw_post_kernel(y_ref, r_ref, k_ref, v_ref, g_ref, rk_ref, lnw_ref, lnb_ref, hp_ref, o_ref):
    hp = hp_ref[...]
    y = y_ref[...]
    inv = 1.0 / HEAD_DIM
    mu = _head_sum(y, hp) * inv
    yc = y - mu
    var = _head_sum(yc * yc, hp) * inv
    yn = yc * lax.rsqrt(var + RW_GN_EPS) * lnw_ref[...] + lnb_ref[...]
    bonus = _head_sum(r_ref[...] * k_ref[...] * rk_ref[...], hp) * v_ref[...]
    o_ref[...] = ((yn + bonus) * g_ref[...]).astype(o_ref.dtype)


def _rw_post(y, r, k, v, g, r_k, ln_w, ln_b, *, tm=256):
    T, D = y.shape
    row = lambda i: (i, 0)
    const = lambda i: (0, 0)
    return pl.pallas_call(
        _rw_post_kernel,
        grid=(T // tm,),
        in_specs=[pl.BlockSpec((tm, D), row)] * 5 + [pl.BlockSpec((1, D), const)] * 3
        + [pl.BlockSpec((2 * HS_W, HS_W), const)],
        out_specs=pl.BlockSpec((tm, D), row),
        out_shape=jax.ShapeDtypeStruct((T, D), BF16),
        compiler_params=_cparams(("parallel",)),
        name="rw_post",
    )(y, r, k, v, g, r_k.reshape(1, D), ln_w[None, :], ln_b[None, :], _head_sum_matrix())


def _rwkv_mixer(x, gn, B, S, mix, w_rkv, w0, w1, w2, a0, a1, a2, g1, g2, k_k, k_a, r_k, ln_w, ln_b, v_first, v_res):
    r, lw, k, v, kn, b, g = _rw_pre(x, gn, mix, w_rkv, w0, w1, w2, a0, a1, a2, g1, g2, k_k, k_a, v_first, v_res, S)
    y = _rw_scan(r, lw, k, v, kn, b, B, S)
    return _rw_post(y, r, k, v, g, r_k, ln_w, ln_b), v


def _sb_mixer(x, gn, B, S, w_qkv):
    qkv = _norm_proj(x, gn, w_qkv.astype(BF16))
    return _sb_attention(qkv.reshape(B, S, 3 * D_MODEL), B, S).reshape(B * S, D_MODEL)


def _dsa_mixer(x, gn, B, S, w_in, q_norm, k_norm):
    qkv, qcat, kcat, wi, kmx = _dsa_proj(x, gn, w_in, q_norm, k_norm, B, S)
    o = _dsa_attention(qkv.reshape(B, S, 3 * D_MODEL), qcat, kcat, wi, kmx, B, S)
    return o.reshape(B * S, D_MODEL)


def kernel(x, norm_mix, norm_ffn, ffn_w_gu, ffn_w_down, rw_mix, rw_w_rkv, rw_w0, rw_w1, rw_w2, rw_a0, rw_a1, rw_a2, rw_g1, rw_g2, rw_v0, rw_v1, rw_v2, rw_k_k, rw_k_a, rw_r_k, rw_ln_w, rw_ln_b, rw_w_out, sb_w_qkv, sb_w_out, ds_w_in, ds_q_norm, ds_k_norm, ds_w_out):
    B, S, D = x.shape
    depth = norm_mix.shape[0]
    xf = x.reshape(B * S, D)
    v_first = None
    for i in range(depth):
        kind, j = i % 3, i // 3
        gn = norm_mix[i][None, :]
        if kind == 0:
            v_res = None if j == 0 else (rw_v0[j - 1], rw_v1[j - 1], rw_v2[j - 1])
            o, v_layer = _rwkv_mixer(xf, gn, B, S, rw_mix[j], rw_w_rkv[j], rw_w0[j], rw_w1[j], rw_w2[j],
                                     rw_a0[j], rw_a1[j], rw_a2[j], rw_g1[j], rw_g2[j], rw_k_k[j], rw_k_a[j],
                                     rw_r_k[j], rw_ln_w[j], rw_ln_b[j], v_first, v_res)
            if j == 0:
                v_first = v_layer
            w_out = rw_w_out[j]
        elif kind == 1:
            o = _sb_mixer(xf, gn, B, S, sb_w_qkv[j])
            w_out = sb_w_out[j]
        else:
            o = _dsa_mixer(xf, gn, B, S, ds_w_in[j], ds_q_norm[j], ds_k_norm[j])
            w_out = ds_w_out[j]
        xf = _ffn_layer(xf, o, w_out.astype(BF16), norm_ffn[i][None, :], ffn_w_gu[i].astype(BF16),
                        ffn_w_down[i].astype(BF16))
    return xf.reshape(B, S, D)
```

```python
import functools
import math

import jax
import jax.numpy as jnp
from jax import lax
from jax.experimental import pallas as pl
from jax.experimental.pallas import tpu as pltpu

F32 = jnp.float32
BF16 = jnp.bfloat16
I32 = jnp.int32
I16 = jnp.int16

D_MODEL = 1024
HEAD_DIM = 64
N_HEADS = D_MODEL // HEAD_DIM
N_PAIRS = N_HEADS // 2
LANES = 128
RMS_EPS = 1e-6
ROPE_THETA = 10000.0
RW_GN_EPS = HEAD_DIM * 1e-5
RW_CHUNK = 64
DS_TOPK_MAX = 256
DS_IDX_HEADS = 8
DS_IDX_DIM = 64
DS_CHUNK = 64
NEG_BIG = -1e30
SB_DEAD_LOG = -105.0
DS_SAFE_LOGIT = 35.0
VMEM_LIMIT = 56 * 1024 * 1024


def _cparams(sem):
    return pltpu.CompilerParams(dimension_semantics=sem, vmem_limit_bytes=VMEM_LIMIT)


def _dot(a, b):
    return jnp.dot(a, b, preferred_element_type=F32)


def _dot_nt(a, b):
    return lax.dot_general(a, b, (((1,), (1,)), ((), ())), preferred_element_type=F32)


def _dot_tn(a, b):
    return lax.dot_general(a, b, (((0,), (0,)), ((), ())), preferred_element_type=F32)


def _split2(x):
    hi = x.astype(BF16)
    lo = (x - hi.astype(F32)).astype(BF16)
    return hi, lo


def _split3(x):
    hi = x.astype(BF16)
    r1 = x - hi.astype(F32)
    mid = r1.astype(BF16)
    lo = (r1 - mid.astype(F32)).astype(BF16)
    return hi, mid, lo


HS_W = 256


def _head_sum(x, hp):
    hi, lo = _split2(x)
    outs = []
    for s in range(x.shape[1] // HS_W):
        sl = slice(s * HS_W, (s + 1) * HS_W)
        outs.append(_dot(jnp.concatenate([hi[:, sl], lo[:, sl]], axis=1), hp))
    return jnp.concatenate(outs, axis=1)


def _rms(x, g):
    ms = jnp.mean(x * x, axis=-1, keepdims=True)
    return x * lax.rsqrt(ms + RMS_EPS) * g


def _ffn_kernel(x_ref, o_ref, wo_ref, g_ref, wgu_ref, wd_ref, out_ref, x1_ref, hn_ref, acc_ref, act_ref, *, tf):
    F = wd_ref.shape[0]
    nf = F // tf
    x1 = x_ref[...] + _dot(o_ref[...], wo_ref[...])
    x1_ref[...] = x1
    hn_ref[...] = _rms(x1, g_ref[...]).astype(BF16)
    for f in range(nf + 1):
        if f < nf:
            hn = hn_ref[...]
            g = _dot(hn, wgu_ref[:, f * tf:(f + 1) * tf])
            u = _dot(hn, wgu_ref[:, F + f * tf:F + (f + 1) * tf])
        if f > 0:
            d = _dot(act_ref[(f - 1) % 2], wd_ref[(f - 1) * tf:f * tf, :])
            if f == 1:
                acc_ref[...] = d
            else:
                acc_ref[...] += d
        if f < nf:
            act_ref[f % 2] = (g * jax.nn.sigmoid(g) * u).astype(BF16)
    out_ref[...] = x1_ref[...] + acc_ref[...]


def _ffn_layer(x, o, w_out, g_ffn, w_gu, w_down, *, tm=512, tf=256):
    T, D = x.shape
    F = w_down.shape[0]
    assert T % tm == 0 and F % tf == 0
    row = lambda i: (i, 0)
    const = lambda i: (0, 0)
    resident = lambda a: pl.BlockSpec(a.shape, const, pipeline_mode=pl.Buffered(1))
    return pl.pallas_call(
        functools.partial(_ffn_kernel, tf=tf),
        grid=(T // tm,),
        in_specs=[
            pl.BlockSpec((tm, D), row),
            pl.BlockSpec((tm, D), row),
            resident(w_out),
            pl.BlockSpec((1, D), const),
            resident(w_gu),
            resident(w_down),
        ],
        out_specs=pl.BlockSpec((tm, D), row),
        out_shape=jax.ShapeDtypeStruct((T, D), F32),
        scratch_shapes=[pltpu.VMEM((tm, D), F32), pltpu.VMEM((tm, D), BF16), pltpu.VMEM((tm, D), F32),
                        pltpu.VMEM((2, tm, tf), BF16)],
        compiler_params=_cparams(("parallel",)),
        name="ffn",
    )(x, o, w_out, g_ffn, w_gu, w_down)


def _norm_proj_kernel(x_ref, g_ref, w_ref, out_ref, hn_ref):
    @pl.when(pl.program_id(1) == 0)
    def _():
        hn_ref[...] = _rms(x_ref[...], g_ref[...]).astype(BF16)

    out_ref[...] = _dot(hn_ref[...], w_ref[...]).astype(out_ref.dtype)


def _norm_proj(x, g, w, *, tm=512, tn=1024):
    T, D = x.shape
    N = w.shape[1]
    return pl.pallas_call(
        _norm_proj_kernel,
        grid=(T // tm, N // tn),
        in_specs=[
            pl.BlockSpec((tm, D), lambda i, j: (i, 0)),
            pl.BlockSpec((1, D), lambda i, j: (0, 0)),
            pl.BlockSpec((D, tn), lambda i, j: (0, j)),
        ],
        out_specs=pl.BlockSpec((tm, tn), lambda i, j: (i, j)),
        out_shape=jax.ShapeDtypeStruct((T, N), BF16),
        scratch_shapes=[pltpu.VMEM((tm, D), BF16)],
        compiler_params=_cparams(("parallel", "arbitrary")),
        name="norm_proj",
    )(x, g, w)


def _sb_kernel(q_ref, k_ref, v_ref, suo_ref, o_ref, c_ref, a_ref, zs_ref, hl_ref, *, tq, tk):
    qi = pl.program_id(2)
    lane = lax.broadcasted_iota(I32, (1, LANES), 1)
    first = lane < HEAD_DIM
    q = q_ref[...] * jnp.asarray(HEAD_DIM ** -0.5, BF16)
    zero = jnp.zeros_like(q)
    qh = (jnp.where(first, q, zero), jnp.where(first, zero, q))
    suo = suo_ref[...]
    c_ref[...] = jnp.zeros_like(c_ref)
    a_ref[...] = jnp.zeros_like(a_ref)
    qpos = qi * tq + lax.broadcasted_iota(I32, (tq, 1), 0)
    def tile(ks, w, masked):
        nsub = w // tk
        ks = pl.multiple_of(ks, tq)
        kb = k_ref[pl.ds(ks, w), :]
        zz = [_dot_nt(qh[h], kb) for h in range(2)]
        if masked:
            earlier = (ks + lax.broadcasted_iota(I32, (1, w), 1)) < qpos
        for h in range(2):
            z = zz[h]
            sp = jnp.maximum(z, 0.0) + jnp.log(1.0 + jnp.exp(-jnp.abs(z)))
            zs = z - sp
            if masked:
                sp = jnp.where(earlier, sp, 0.0)
                zs = jnp.where(earlier, zs, NEG_BIG)
            zs_ref[h, :, :w] = zs
            for u in range(nsub):
                hi, lo = _split2(sp[:, u * tk:(u + 1) * tk])
                hl_ref[h, :, 2 * u * tk:(2 * u + 1) * tk] = hi
                hl_ref[h, :, (2 * u + 1) * tk:(2 * u + 2) * tk] = lo
        cs = [[_dot(hl_ref[h, :, 2 * u * tk:(2 * u + 2) * tk], suo) for u in range(nsub)] for h in range(2)]
        vb = v_ref[pl.ds(ks, w), :]
        zv = jnp.zeros_like(vb)
        vb2 = jnp.concatenate([jnp.where(first, vb, zv), jnp.where(first, zv, vb)], axis=0)
        att = []
        for h in range(2):
            c = c_ref[h]
            parts = [None] * nsub
            for u in reversed(range(nsub)):
                a = jnp.exp(zs_ref[h, :, u * tk:(u + 1) * tk] + cs[h][u][:, :tk] + c)
                parts[u] = a.astype(BF16)
                c = c + cs[h][u][:, tk:]
            c_ref[h] = c
            att += parts
        a_ref[...] += _dot(jnp.concatenate(att, axis=1), vb2)

    def c_max():
        return jnp.max(jnp.maximum(c_ref[0], c_ref[1]))

    s0 = jnp.maximum(qi - 1, 0)
    tile(s0 * tq, 2 * tq, True)

    def keep_going(carry):
        t, cm = carry
        return jnp.logical_and(t < s0, cm > SB_DEAD_LOG)

    def step(carry):
        t, _ = carry
        tile((s0 - 1 - t) * tq, tq, False)
        return t + 1, c_max()

    lax.while_loop(keep_going, step, (jnp.int32(0), c_max()))

    o_ref[...] = a_ref[...].astype(o_ref.dtype)


def _sb_attention(qkv, B, S, *, tq=256, tk=128):
    D = D_MODEL
    r = lax.broadcasted_iota(I32, (tk, 2 * tk), 0)
    c = lax.broadcasted_iota(I32, (tk, 2 * tk), 1)
    suo = jnp.where((c >= tk) | (r > c), -1.0, 0.0).astype(BF16)
    suo = jnp.concatenate([suo, suo], axis=0)
    kern = functools.partial(_sb_kernel, tq=tq, tk=tk)
    return pl.pallas_call(
        kern,
        grid=(B, N_PAIRS, S // tq),
        in_specs=[
            pl.BlockSpec((None, tq, LANES), lambda b, p, i: (b, i, p)),
            pl.BlockSpec((None, S, LANES), lambda b, p, i: (b, 0, N_PAIRS + p)),
            pl.BlockSpec((None, S, LANES), lambda b, p, i: (b, 0, 2 * N_PAIRS + p)),
            pl.BlockSpec((2 * tk, 2 * tk), lambda b, p, i: (0, 0)),
        ],
        out_specs=pl.BlockSpec((None, tq, LANES), lambda b, p, i: (b, i, p)),
        out_shape=jax.ShapeDtypeStruct((B, S, D), BF16),
        scratch_shapes=[pltpu.VMEM((2, tq, tk), F32), pltpu.VMEM((tq, LANES), F32),
                        pltpu.VMEM((2, tq, 2 * tq), F32), pltpu.VMEM((2, tq, 4 * tq), BF16)],
        compiler_params=_cparams(("parallel", "parallel", "arbitrary")),
        name="sb_attn",
    )(qkv, qkv, qkv, suo)


def _rope_slabs(y, cos, sin):
    lane = lax.broadcasted_iota(I32, (1, LANES), 1)
    first = (lane % HEAD_DIM) < (HEAD_DIM // 2)
    outs = []
    for s in range(y.shape[1] // LANES):
        ys = y[:, s * LANES:(s + 1) * LANES]
        rot = jnp.where(first, pltpu.roll(ys, LANES - HEAD_DIM // 2, axis=1), pltpu.roll(ys, HEAD_DIM // 2, axis=1))
        outs.append(ys * cos[:, s * LANES:(s + 1) * LANES] + rot * sin[:, s * LANES:(s + 1) * LANES])
    return jnp.concatenate(outs, axis=1)


def _dsa_proj_kernel(x_ref, g_ref, w_ref, wih_ref, wil_ref, hp_ref, qn_ref, kn_ref, cos_ref, sin_ref,
                     qkv_ref, qcat_ref, kcat_ref, wi_ref, kmx_ref, hh_ref, hl_ref, *, wi_scale):
    j = pl.program_id(1)
    nq = DS_IDX_HEADS * DS_IDX_DIM

    @pl.when(j == 0)
    def _():
        hn = _rms(x_ref[...], g_ref[...])
        hh, hl = _split2(hn)
        hh_ref[...] = hh
        hl_ref[...] = hl
        wih = wih_ref[...]
        idx = _dot(hh, wih) + _dot(hl, wih) + _dot(hh, wil_ref[...])
        cos = cos_ref[...]
        sin = sin_ref[...]
        lane = lax.broadcasted_iota(I32, (1, LANES), 1)
        first = lane < DS_IDX_DIM

        def hi_lo(x):
            hi = x.astype(BF16).astype(F32)
            return hi, x - hi

        qi = _rope_slabs(idx[:, :nq], cos[:, :nq], sin[:, :nq])
        qhi, qlo = hi_lo(qi)
        for h in range(DS_IDX_HEADS):
            sl = slice((h // 2) * LANES, (h // 2 + 1) * LANES)
            if h % 2 == 0:
                own_hi = qhi[:, sl]
                a = jnp.where(first, own_hi, pltpu.roll(qlo[:, sl], DS_IDX_DIM, axis=1))
            else:
                own_hi = pltpu.roll(qhi[:, sl], DS_IDX_DIM, axis=1)
                a = jnp.where(first, own_hi, qlo[:, sl])
            b = jnp.where(first, own_hi, 0.0)
            qcat_ref[:, 2 * h * LANES:(2 * h + 1) * LANES] = a.astype(BF16)
            qcat_ref[:, (2 * h + 1) * LANES:(2 * h + 2) * LANES] = b.astype(BF16)
        tail = idx[:, nq:nq + LANES]
        ki = jnp.where(first, _rope_slabs(tail, cos[:, :LANES], sin[:, :LANES]), 0.0)
        khi, klo = hi_lo(ki)
        kcat_ref[:, :LANES] = (khi + pltpu.roll(khi, DS_IDX_DIM, axis=1)).astype(BF16)
        kcat_ref[:, LANES:] = klo.astype(BF16)
        wi_ref[...] = tail * wi_scale

    y = _dot(hh_ref[...], w_ref[...])

    def normed(gn_ref, scale):
        ms = _head_sum(y * y, hp_ref[...]) * (1.0 / HEAD_DIM)
        yn = y * lax.rsqrt(ms + RMS_EPS) * gn_ref[...]
        out = _rope_slabs(yn, cos_ref[...], sin_ref[...])
        return (out * scale).astype(qkv_ref.dtype)

    @pl.when(j == 0)
    def _():
        qkv_ref[...] = normed(qn_ref, HEAD_DIM ** -0.5)

    @pl.when(j == 1)
    def _():
        kb = normed(kn_ref, 1.0)
        qkv_ref[...] = kb
        kf = kb.astype(F32)
        kmx_ref[...] = jnp.max(_head_sum(kf * kf, hp_ref[...]), axis=0, keepdims=True)

    @pl.when(j == 2)
    def _():
        qkv_ref[...] = y.astype(qkv_ref.dtype)


def _head_sum_matrix():
    r = (lax.broadcasted_iota(I32, (2 * HS_W, HS_W), 0) % HS_W) // HEAD_DIM
    c = lax.broadcasted_iota(I32, (2 * HS_W, HS_W), 1) // HEAD_DIM
    return jnp.where(r == c, 1.0, 0.0).astype(BF16)


def _rope_tables(S):
    half = HEAD_DIM // 2
    inv = 1.0 / (ROPE_THETA ** (jnp.arange(half, dtype=F32) / half))
    ang = jnp.arange(S, dtype=F32)[:, None] * inv[None, :]
    cos = jnp.concatenate([jnp.cos(ang), jnp.cos(ang)], axis=1)
    sin = jnp.concatenate([-jnp.sin(ang), jnp.sin(ang)], axis=1)
    return jnp.tile(cos, (1, N_HEADS)), jnp.tile(sin, (1, N_HEADS))


def _dsa_proj(x, g, w_in, q_norm, k_norm, B, S, *, tm=256):
    T, D = x.shape
    c3 = 3 * D
    nq = DS_IDX_HEADS * DS_IDX_DIM
    n_idx = nq + DS_IDX_DIM + DS_IDX_HEADS
    w_qkv = w_in[:, :c3].astype(BF16)
    w_idx = jnp.pad(w_in[:, c3:], ((0, 0), (0, nq + LANES - n_idx)))
    wih = w_idx.astype(BF16)
    wil = (w_idx - wih.astype(F32)).astype(BF16)
    cos, sin = _rope_tables(S)
    qn = jnp.tile(q_norm, N_HEADS)[None, :]
    kn = jnp.tile(k_norm, N_HEADS)[None, :]
    wi_scale = DS_IDX_HEADS ** -0.5 * DS_IDX_DIM ** -0.5
    nsb = S // tm
    row = lambda i, j: (i, 0)
    const = lambda i, j: (0, 0)
    pos = lambda i, j: (i % nsb, 0)
    return pl.pallas_call(
        functools.partial(_dsa_proj_kernel, wi_scale=wi_scale),
        grid=(T // tm, 3),
        in_specs=[
            pl.BlockSpec((tm, D), row),
            pl.BlockSpec((1, D), const),
            pl.BlockSpec((D, D), lambda i, j: (0, j)),
            pl.BlockSpec((D, nq + LANES), const),
            pl.BlockSpec((D, nq + LANES), const),
            pl.BlockSpec((2 * HS_W, HS_W), const),
            pl.BlockSpec((1, D), const),
            pl.BlockSpec((1, D), const),
            pl.BlockSpec((tm, D), pos),
            pl.BlockSpec((tm, D), pos),
        ],
        out_specs=[
            pl.BlockSpec((tm, D), lambda i, j: (i, j)),
            pl.BlockSpec((tm, 2 * LANES * DS_IDX_HEADS), row),
            pl.BlockSpec((tm, 2 * LANES), row),
            pl.BlockSpec((tm, LANES), row),
            pl.BlockSpec((None, 1, D), lambda i, j: (i, 0, 0)),
        ],
        out_shape=[
            jax.ShapeDtypeStruct((T, c3), BF16),
            jax.ShapeDtypeStruct((T, 2 * LANES * DS_IDX_HEADS), BF16),
            jax.ShapeDtypeStruct((T, 2 * LANES), BF16),
            jax.ShapeDtypeStruct((T, LANES), F32),
            jax.ShapeDtypeStruct((T // tm, 1, D), F32),
        ],
        scratch_shapes=[pltpu.VMEM((tm, D), BF16), pltpu.VMEM((tm, D), BF16)],
        compiler_params=_cparams(("parallel", "arbitrary")),
        name="dsa_proj",
    )(x, g, w_qkv, wih, wil, _head_sum_matrix(), qn, kn, cos, sin)


def _dsa_attn_kernel(q_ref, k_ref, v_ref, qcat_ref, kcat_ref, wi_ref, kmx_ref, tri_ref, o_ref,
                     keys_ref, kh_ref, bias_ref, thr_ref, need_ref, eqb_ref, m_ref, mx_ref, acc_ref,
                     *, tq, ta, rb, n_sel):
    i = pl.program_id(1)
    p = pl.program_id(2)
    lane = lax.broadcasted_iota(I32, (1, LANES), 1)
    row = lax.broadcasted_iota(I32, (tq, 1), 0)
    vis_row = i * tq + (row // DS_CHUNK + 1) * DS_CHUNK
    vis_tile = (i + 1) * tq
    n_a = (vis_tile + ta - 1) // ta

    @pl.when(p == 0)
    def _select():
        wi = wi_ref[...]

        def score_blk(c, _):
            cs = pl.multiple_of(c * ta, ta)
            kc = kcat_ref[pl.ds(cs, ta), :]
            idxs = [_dot_nt(qcat_ref[:, 2 * h * LANES:(2 * h + 2) * LANES], kc) for h in range(DS_IDX_HEADS)]
            sc = jnp.zeros((tq, ta), F32)
            for h in range(DS_IDX_HEADS):
                sc = sc + wi[:, DS_IDX_DIM + h:DS_IDX_DIM + h + 1] * jnp.maximum(idxs[h], 0.0)
            sc = jnp.where(sc == 0.0, 0.0, sc)
            kpos = cs + lax.broadcasted_iota(I32, (1, ta), 1)
            sc = jnp.where(kpos < vis_row, sc, -jnp.inf)
            bits = lax.bitcast_convert_type(sc, I32)
            keys = jnp.where(bits < 0, bits ^ jnp.int32(0x7FFFFFFF), bits)
            keys_ref[:, pl.ds(cs, ta)] = keys
            kh_ref[:, pl.ds(cs, ta)] = lax.shift_right_arithmetic(keys, 16).astype(I16)
            return 0

        int_min = jnp.iinfo(jnp.int32).min
        lax.fori_loop(0, n_a, score_blk, 0)

        @pl.when(n_a % 2 == 1)
        def _():
            pad = pl.ds(pl.multiple_of(n_a * ta, ta), ta)
            keys_ref[:, pad] = jnp.full((tq, ta), int_min, I32)
            kh_ref[:, pad] = jnp.full((tq, ta), jnp.iinfo(jnp.int16).min, I16)

        nsweep = 2 * ta // LANES
        groups = [slice(g * rb, (g + 1) * rb) for g in range(tq // rb)]

        def count(cmps, strict):
            outs = []
            for g0 in range(0, len(groups), 2):
                def blk(c, parts, g0=g0):
                    cs = pl.multiple_of(c * 2 * ta, 2 * ta)
                    parts = list(parts)
                    for s in range(nsweep):
                        for d in range(2):
                            kk = keys_ref[groups[g0 + d], pl.ds(cs + s * LANES, LANES)]
                            hit = (kk > cmps[g0 + d]) if strict else (kk >= cmps[g0 + d])
                            parts[d] = parts[d] + jnp.where(hit, 1, 0).astype(I32)
                    return tuple(parts)
                zero = jnp.zeros((rb, LANES), I32)
                outs += list(lax.fori_loop(0, (n_a + 1) // 2, blk, (zero, zero)))
            return [jnp.sum(pt, axis=1, keepdims=True) for pt in outs]

        def count_upper(cmps):
            one = jnp.ones((rb, LANES), I16)
            zero = jnp.zeros((rb, LANES), I16)

            def blk(c, parts):
                cs = pl.multiple_of(c * 2 * ta, 2 * ta)
                parts = list(parts)
                for s in range(nsweep):
                    for g in range(len(groups)):
                        kk = kh_ref[groups[g], pl.ds(cs + s * LANES, LANES)]
                        parts[g] = parts[g] + jnp.where(kk >= cmps[g], one, zero)
                return tuple(parts)

            parts = lax.fori_loop(0, (n_a + 1) // 2, blk, (zero,) * len(groups))
            return [jnp.sum(pt.astype(I32), axis=1, keepdims=True) for pt in parts]

        thr_ref[...] = jnp.full(thr_ref.shape, int_min, I32)

        def bit_step(t, _, upper):
            bit = lax.shift_left(jnp.int32(1), 31 - t)
            cands = [thr_ref[g, :] + bit for g in groups]
            if upper:
                cnts = count_upper([lax.shift_right_arithmetic(cand, 16).astype(I16) for cand in cands])
            else:
                cnts = count(cands, False)
            for g, cand, cnt in zip(groups, cands, cnts):
                thr_ref[g, :] = jnp.where(cnt >= n_sel, cand, thr_ref[g, :])
            return 0

        lax.fori_loop(0, 16, functools.partial(bit_step, upper=True), 0)
        lax.fori_loop(16, 32, functools.partial(bit_step, upper=False), 0)
        nrep = ta // LANES
        n_ge = count([thr_ref[g, :] for g in groups], False)
        tied = sum(jnp.max(jnp.abs(cnt - n_sel)) for cnt in n_ge) > 0

        @pl.when(jnp.logical_not(tied))
        def _():
            def bias_blk(c, _):
                cs = pl.multiple_of(c * ta, ta)
                kk = keys_ref[:, pl.ds(cs, ta)]
                thr = jnp.concatenate([thr_ref[...]] * nrep, axis=1)
                kpos = cs + lax.broadcasted_iota(I32, (1, ta), 1)
                bias_ref[:, pl.ds(cs, ta)] = jnp.where(kpos < vis_row, jnp.where(kk >= thr, 0.0, NEG_BIG), NEG_BIG)
                return 0

            lax.fori_loop(0, n_a, bias_blk, 0)

        @pl.when(tied)
        def _():
            for g, cnt in zip(groups, count([thr_ref[g, :] for g in groups], True)):
                need_ref[g, :] = jnp.broadcast_to((n_sel - cnt).astype(F32), (rb, LANES))
            tri = tri_ref[...]

            def bias_blk(c, _):
                cs = pl.multiple_of(c * ta, ta)
                kk = keys_ref[:, pl.ds(cs, ta)]
                thr = jnp.concatenate([thr_ref[...]] * nrep, axis=1)
                eqf = jnp.where(kk == thr, 1.0, 0.0)
                ranks = _dot(eqf.astype(BF16), tri)
                rank = jnp.concatenate([eqb_ref[...]] * nrep, axis=1) + ranks[:, :ta]
                need = jnp.concatenate([need_ref[...]] * nrep, axis=1)
                sel = jnp.where(kk > thr, 1.0, jnp.where(rank < need, eqf, 0.0))
                kpos = cs + lax.broadcasted_iota(I32, (1, ta), 1)
                sel = jnp.where(kpos < vis_row, sel, 0.0)
                bias_ref[:, pl.ds(cs, ta)] = (sel - 1.0) * (-NEG_BIG)
                eqb_ref[...] += ranks[:, ta:]
                return 0

            eqb_ref[...] = jnp.zeros_like(eqb_ref)
            lax.fori_loop(0, n_a, bias_blk, 0)

        @pl.when(n_a % 2 == 1)
        def _():
            bias_ref[:, pl.ds(pl.multiple_of(n_a * ta, ta), ta)] = jnp.full((tq, ta), NEG_BIG, F32)

    q = q_ref[...]
    zero = jnp.zeros_like(q)
    first = lane < HEAD_DIM
    qh2 = (jnp.where(first, q, zero), jnp.where(first, zero, q))
    n_pair = (n_a + 1) // 2

    trow = lax.broadcasted_iota(I32, (kmx_ref.shape[0], 1), 0)
    kmax2 = jnp.max(jnp.where(trow <= i, kmx_ref[...], 0.0), axis=0, keepdims=True)
    hr = lax.broadcasted_iota(I32, (LANES, LANES), 0) // HEAD_DIM
    hc = lax.broadcasted_iota(I32, (LANES, LANES), 1) // HEAD_DIM
    qf = q.astype(F32)
    q2 = _dot((qf * qf).astype(BF16), jnp.where(hr == hc, 1.0, 0.0).astype(BF16))
    bound2 = 1.05 * jnp.max(q2 * kmax2)
    m_ref[...] = jnp.zeros_like(m_ref)

    @pl.when(bound2 > DS_SAFE_LOGIT * DS_SAFE_LOGIT)
    def _row_max():
        mx_ref[...] = jnp.full(mx_ref.shape, NEG_BIG, F32)

        def max_blk(c2, _):
            for u in range(2):
                cs = pl.multiple_of((2 * c2 + u) * ta, ta)
                kb = k_ref[pl.ds(cs, ta), :]
                bias = bias_ref[:, pl.ds(cs, ta)]
                for h in range(2):
                    mx_ref[u, h] = jnp.maximum(mx_ref[u, h], _dot_nt(qh2[h], kb) + bias)
            return 0

        lax.fori_loop(0, n_pair, max_blk, 0)
        for h in range(2):
            m_ref[h] = jnp.broadcast_to(jnp.max(jnp.maximum(mx_ref[0, h], mx_ref[1, h]), axis=1, keepdims=True),
                                        (tq, LANES))

    ones_h = [jnp.broadcast_to(jnp.where(first, a, b), (ta, LANES)).astype(BF16) for a, b in ((1.0, 0.0), (0.0, 1.0))]
    acc_ref[...] = jnp.zeros_like(acc_ref)

    def attn_blocks(c0, nblk):
        css = [pl.multiple_of((c0 + u) * ta, ta) for u in range(nblk)]
        logits = [[_dot_nt(qh2[h], k_ref[pl.ds(cs, ta), :]) for h in range(2)] for cs in css]
        out = acc_ref[...]
        for u, cs in enumerate(css):
            vb = v_ref[pl.ds(cs, ta), :]
            zv = jnp.zeros_like(vb)
            vs = [jnp.concatenate([jnp.where(first, vb, zv), ones_h[0]], axis=1),
                  jnp.concatenate([jnp.where(first, zv, vb), ones_h[1]], axis=1)]
            bias = bias_ref[:, pl.ds(cs, ta)]
            for h in range(2):
                pr = jnp.exp(logits[u][h] + bias - m_ref[h][:, :1]).astype(BF16)
                out = out + _dot(pr, vs[h])
        acc_ref[...] = out

    n_quad = n_a // 4

    def quad(c4, _):
        attn_blocks(4 * c4, 4)
        return 0

    lax.fori_loop(0, n_quad, quad, 0)
    rest = n_a - 4 * n_quad

    @pl.when(rest > 0)
    def _():
        attn_blocks(4 * n_quad, 2)

    @pl.when(rest == 3)
    def _():
        attn_blocks(4 * n_quad + 2, 2)
    acc = acc_ref[...]
    o_ref[...] = (acc[:, :LANES] / acc[:, LANES:]).astype(o_ref.dtype)


def _dsa_attention(qkv, qcat, kcat, wi, kmx, B, S, *, tq=256, ta=512, rb=64):
    D = D_MODEL
    n_sel = min(DS_TOPK_MAX, S // 4)
    assert S % (2 * ta) == 0 and S // LANES <= 256 and kmx.shape[0] * tq == B * S
    nqc = 2 * LANES * DS_IDX_HEADS
    r = lax.broadcasted_iota(I32, (ta, ta + LANES), 0)
    c = lax.broadcasted_iota(I32, (ta, ta + LANES), 1)
    tri = jnp.where((r < c) | (c >= ta), 1.0, 0.0).astype(BF16)
    kern = functools.partial(_dsa_attn_kernel, tq=tq, ta=ta, rb=rb, n_sel=n_sel)
    s_pad = S
    return pl.pallas_call(
        kern,
        grid=(B, S // tq, N_PAIRS),
        in_specs=[
            pl.BlockSpec((None, tq, LANES), lambda b, i, p: (b, i, p)),
            pl.BlockSpec((None, S, LANES), lambda b, i, p: (b, 0, N_PAIRS + p)),
            pl.BlockSpec((None, S, LANES), lambda b, i, p: (b, 0, 2 * N_PAIRS + p)),
            pl.BlockSpec((None, tq, nqc), lambda b, i, p: (b, i, 0)),
            pl.BlockSpec((None, S, 2 * LANES), lambda b, i, p: (b, 0, 0)),
            pl.BlockSpec((None, tq, LANES), lambda b, i, p: (b, i, 0)),
            pl.BlockSpec((None, S // tq, LANES), lambda b, i, p: (b, 0, p)),
            pl.BlockSpec((ta, ta + LANES), lambda b, i, p: (0, 0)),
        ],
        out_specs=pl.BlockSpec((None, tq, LANES), lambda b, i, p: (b, i, p)),
        out_shape=jax.ShapeDtypeStruct((B, S, D), BF16),
        scratch_shapes=[
            pltpu.VMEM((tq, s_pad), I32),
            pltpu.VMEM((tq, s_pad), I16),
            pltpu.VMEM((tq, s_pad), F32),
            pltpu.VMEM((tq, LANES), I32),
            pltpu.VMEM((tq, LANES), F32),
            pltpu.VMEM((tq, LANES), F32),
            pltpu.VMEM((2, tq, LANES), F32),
            pltpu.VMEM((2, 2, tq, ta), F32),
            pltpu.VMEM((tq, 2 * LANES), F32),
        ],
        compiler_params=_cparams(("parallel", "arbitrary", "arbitrary")),
        name="dsa_attn",
    )(qkv, qkv, qkv, qcat.reshape(B, S, nqc), kcat.reshape(B, S, 2 * LANES), wi.reshape(B, S, LANES),
      kmx.reshape(B, S // tq, D), tri)


def _rw_pre_kernel(*refs, has_vres, seq_tiles):
    if has_vres:
        (x_ref, xp_ref, gn_ref, mix_ref, wr_ref, wk_ref, wv_ref, w0_ref, w1_ref, w2_ref, a0_ref, a1_ref, a2_ref,
         g1_ref, g2_ref, kk_ref, ka_ref, hp_ref, vf_ref, v0_ref, v1_ref, v2_ref,
         r_out, lw_out, k_out, v_out, kn_out, b_out, g_out) = refs
    else:
        (x_ref, xp_ref, gn_ref, mix_ref, wr_ref, wk_ref, wv_ref, w0_ref, w1_ref, w2_ref, a0_ref, a1_ref, a2_ref,
         g1_ref, g2_ref, kk_ref, ka_ref, hp_ref,
         r_out, lw_out, k_out, v_out, kn_out, b_out, g_out) = refs
    i = pl.program_id(0)
    gn = gn_ref[...]
    h = _rms(x_ref[...], gn)
    tm = h.shape[0]
    hprev = _rms(xp_ref[...], gn)[7:8, :]
    hprev = jnp.where(i % seq_tiles == 0, 0.0, hprev)
    rowi = lax.broadcasted_iota(I32, (tm, 1), 0)
    sh = jnp.where(rowi == 0, hprev, pltpu.roll(h, 1, axis=0))
    dlt = sh - h
    mix = mix_ref[...]

    def stream(j):
        return (h + dlt * mix[j:j + 1, :]).astype(BF16)

    r = _dot(stream(0), wr_ref[...])
    k = _dot(stream(1), wk_ref[...])
    xv = stream(2)
    v = _dot(xv, wv_ref[...])
    w = w0_ref[...] + _dot(jnp.tanh(_dot(stream(3), w1_ref[...])).astype(BF16), w2_ref[...])
    a = jax.nn.sigmoid(a0_ref[...] + _dot(_dot(stream(4), a1_ref[...]).astype(BF16), a2_ref[...]))
    g = _dot(jax.nn.sigmoid(_dot(stream(5), g1_ref[...])).astype(BF16), g2_ref[...])
    if has_vres:
        gate = jax.nn.sigmoid(v0_ref[...] + _dot(_dot(xv, v1_ref[...]).astype(BF16), v2_ref[...]))
        v = v + (vf_ref[...] - v) * gate
    logw = -(jnp.maximum(-w, 0.0) + jnp.log1p(jnp.exp(-jnp.abs(w)))) - 0.5
    lw_out[...] = -jnp.exp(logw)
    kk = k * kk_ref[...]
    ss = _head_sum(kk * kk, hp_ref[...])
    kk = kk * lax.rsqrt(jnp.maximum(ss, 1e-24))
    r_out[...] = r
    k_out[...] = k * (1.0 + (a - 1.0) * ka_ref[...])
    v_out[...] = v
    kn_out[...] = kk
    b_out[...] = kk * a
    g_out[...] = g


def _pad_cols(w, n):
    return jnp.pad(w, ((0, 0), (0, n - w.shape[1])))


def _pad_rows(w, n):
    return jnp.pad(w, ((0, n - w.shape[0]), (0, 0)))


def _rw_pre(x, gn, mix, w_rkv, w0, w1, w2, a0, a1, a2, g1, g2, k_k, k_a, v_first, v_res, S, *, tm=256):
    T, D = x.shape
    has_vres = v_res is not None
    row = lambda i: (i, 0)
    const = lambda i: (0, 0)

    def lora(wa, wb):
        n = -(-wa.shape[1] // LANES) * LANES
        return _pad_cols(wa, n).astype(BF16), _pad_rows(wb, n).astype(BF16)

    w1p, w2p = lora(w1, w2)
    a1p, a2p = lora(a1, a2)
    g1p, g2p = lora(g1, g2)
    args = [x, x, gn, mix, w_rkv[0].astype(BF16), w_rkv[1].astype(BF16), w_rkv[2].astype(BF16),
            w0[None, :], w1p, w2p, a0[None, :], a1p, a2p, g1p, g2p, k_k[None, :], k_a[None, :], _head_sum_matrix()]
    full = lambda a: pl.BlockSpec(a.shape, const)
    in_specs = [pl.BlockSpec((tm, D), row),
                pl.BlockSpec((8, D), lambda i: (jnp.maximum(i * (tm // 8) - 1, 0), 0))]
    in_specs += [full(a) for a in args[2:]]
    if has_vres:
        v0, v1, v2 = v_res
        v1p, v2p = lora(v1, v2)
        extra = [v_first, v0[None, :], v1p, v2p]
        args += extra
        in_specs += [pl.BlockSpec((tm, D), row)] + [full(a) for a in extra[1:]]
    outs = pl.pallas_call(
        functools.partial(_rw_pre_kernel, has_vres=has_vres, seq_tiles=S // tm),
        grid=(T // tm,),
        in_specs=in_specs,
        out_specs=[pl.BlockSpec((tm, D), row)] * 7,
        out_shape=[jax.ShapeDtypeStruct((T, D), F32)] * 7,
        compiler_params=_cparams(("parallel",)),
        name="rw_pre",
    )(*args)
    return outs


def _rw_scan_kernel(r_ref, lw_ref, k_ref, v_ref, kn_ref, b_ref, y_ref, s_ref):
    L = RW_CHUNK

    @pl.when(pl.program_id(1) == 0)
    def _():
        s_ref[...] = jnp.zeros_like(s_ref)

    nb = r_ref.shape[0]

    def rows(ref):
        return jnp.concatenate([ref[s] for s in range(nb)], axis=0)

    lw = rows(lw_ref)
    tr = lax.broadcasted_iota(I32, (nb * L, nb * L), 0)
    tc = lax.broadcasted_iota(I32, (nb * L, nb * L), 1)
    tri = jnp.where((tr >= tc) & (tr // L == tc // L), 1.0, 0.0).astype(BF16)
    hi, mid, lo = _split3(lw)
    lwc = _dot(tri, hi) + _dot(tri, mid) + _dot(tri, lo)
    wl = jnp.concatenate([jnp.broadcast_to(lwc[(s + 1) * L - 1:(s + 1) * L, :], (L, lwc.shape[1]))
                          for s in range(nb)], axis=0)
    w_inc = jnp.exp(lwc)
    w_exc = jnp.exp(lwc - lw)
    w_inv = jnp.exp(-lwc)
    w_end = jnp.exp(wl - lwc)
    w_all = jnp.exp(wl)
    kv = rows(k_ref)
    bv = rows(b_ref)
    vv = rows(v_ref)
    rt = rows(r_ref) * w_inc
    at = -rows(kn_ref) * w_exc
    kt = kv * w_inv
    bt = bv * w_inv
    ke = kv * w_end
    be = bv * w_end

    lane = lax.broadcasted_iota(I32, (1, LANES), 1)
    first = lane < HEAD_DIM

    def sm(x):
        return jnp.concatenate([jnp.where(first, x, 0.0), jnp.where(first, 0.0, x)], axis=0)

    n2 = 2 * L
    ri = lax.broadcasted_iota(I32, (2 * n2, 2 * n2), 0)
    ci = lax.broadcasted_iota(I32, (2 * n2, 2 * n2), 1)
    causal = (ri % L) + ri // n2 > (ci % L)
    bi = lax.broadcasted_iota(I32, (n2, n2), 0) // L
    bj = lax.broadcasted_iota(I32, (n2, n2), 1) // L
    blockdiag = bi == bj
    eye = jnp.where(lax.broadcasted_iota(I32, (n2, n2), 0) == lax.broadcasted_iota(I32, (n2, n2), 1), 1.0, 0.0)

    ents = [(s, p) for s in range(nb) for p in range(N_PAIRS)]
    prs = range(len(ents))
    cut = [(slice(s * L, (s + 1) * L), slice(p * LANES, (p + 1) * LANES)) for s, p in ents]
    vsb = [sm(vv[c]).astype(BF16) for c in cut]
    a_l = [jnp.concatenate([sm(at[c]), sm(rt[c])], axis=0).astype(BF16) for c in cut]
    a_r = [jnp.concatenate([sm(bt[c]), sm(kt[c])], axis=0).astype(BF16) for c in cut]
    m = [jnp.where(causal, _dot_nt(a_l[e], a_r[e]), 0.0) for e in prs]
    s_old = [s_ref[e] for e in prs]
    p12 = [_dot_nt(a_l[e], s_old[e].astype(BF16)) for e in prs]
    pm = [m[e][:n2, :n2].astype(BF16) for e in prs]
    tm = [eye + m[e][:n2, :n2] for e in prs]
    for _ in range(5):
        pm = [_dot(pm[e], pm[e]).astype(BF16) for e in prs]
        tm = [tm[e] + _dot(tm[e].astype(BF16), pm[e]) for e in prs]
    z = [p12[e][:n2] + _dot(m[e][:n2, n2:].astype(BF16), vsb[e]) for e in prs]
    u = [_dot(tm[e].astype(BF16), z[e].astype(BF16)) for e in prs]
    y = [p12[e][n2:] + _dot(m[e][n2:, :].astype(BF16), jnp.concatenate([u[e].astype(BF16), vsb[e]], axis=0))
         for e in prs]
    upd = [_dot_tn(jnp.concatenate([u[e][:L] + u[e][L:], vv[cut[e]]], axis=0).astype(BF16),
                   jnp.concatenate([be[cut[e]], ke[cut[e]]], axis=0).astype(BF16)) for e in prs]
    for e, (s, p) in enumerate(ents):
        y_ref[s, :, cut[e][1]] = y[e][:L] + y[e][L:]
        s_ref[e] = s_old[e] * w_all[s * L:s * L + 1, cut[e][1]] + jnp.where(blockdiag, upd[e], 0.0)


def _rw_scan(r, lw, k, v, kn, b, B, S):
    D = D_MODEL
    L = RW_CHUNK
    nb = 2 if B % 2 == 0 else 1
    spec = pl.BlockSpec((nb, L, D), lambda bb, c: (bb, c, 0))
    args = [a.reshape(B, S, D) for a in (r, lw, k, v, kn, b)]
    y = pl.pallas_call(
        _rw_scan_kernel,
        grid=(B // nb, S // L),
        in_specs=[spec] * 6,
        out_specs=spec,
        out_shape=jax.ShapeDtypeStruct((B, S, D), F32),
        scratch_shapes=[pltpu.VMEM((nb * N_PAIRS, LANES, LANES), F32)],
        compiler_params=_cparams(("parallel", "arbitrary")),
        name="rw_scan",
    )(*args)
    return y.reshape(B * S, D)


def _rw_post_kernel(y_ref, r_ref, k_ref, v_ref, g_ref, rk_ref, lnw_ref, lnb_ref, hp_ref, o_ref):
    hp = hp_ref[...]
    y = y_ref[...]
    inv = 1.0 / HEAD_DIM
    mu = _head_sum(y, hp) * inv
    yc = y - mu
    var = _head_sum(yc * yc, hp) * inv
    yn = yc * lax.rsqrt(var + RW_GN_EPS) * lnw_ref[...] + lnb_ref[...]
    bonus = _head_sum(r_ref[...] * k_ref[...] * rk_ref[...], hp) * v_ref[...]
    o_ref[...] = ((yn + bonus) * g_ref[...]).astype(o_ref.dtype)


def _rw_post(y, r, k, v, g, r_k, ln_w, ln_b, *, tm=256):
    T, D = y.shape
    row = lambda i: (i, 0)
    const = lambda i: (0, 0)
    return pl.pallas_call(
        _rw_post_kernel,
        grid=(T // tm,),
        in_specs=[pl.BlockSpec((tm, D), row)] * 5 + [pl.BlockSpec((1, D), const)] * 3
        + [pl.BlockSpec((2 * HS_W, HS_W), const)],
        out_specs=pl.BlockSpec((tm, D), row),
        out_shape=jax.ShapeDtypeStruct((T, D), BF16),
        compiler_params=_cparams(("parallel",)),
        name="rw_post",
    )(y, r, k, v, g, r_k.reshape(1, D), ln_w[None, :], ln_b[None, :], _head_sum_matrix())


def _rwkv_mixer(x, gn, B, S, mix, w_rkv, w0, w1, w2, a0, a1, a2, g1, g2, k_k, k_a, r_k, ln_w, ln_b, v_first, v_res):
    r, lw, k, v, kn, b, g = _rw_pre(x, gn, mix, w_rkv, w0, w1, w2, a0, a1, a2, g1, g2, k_k, k_a, v_first, v_res, S)
    y = _rw_scan(r, lw, k, v, kn, b, B, S)
    return _rw_post(y, r, k, v, g, r_k, ln_w, ln_b), v


def _sb_mixer(x, gn, B, S, w_qkv):
    qkv = _norm_proj(x, gn, w_qkv.astype(BF16))
    return _sb_attention(qkv.reshape(B, S, 3 * D_MODEL), B, S).reshape(B * S, D_MODEL)


def _dsa_mixer(x, gn, B, S, w_in, q_norm, k_norm):
    qkv, qcat, kcat, wi, kmx = _dsa_proj(x, gn, w_in, q_norm, k_norm, B, S)
    o = _dsa_attention(qkv.reshape(B, S, 3 * D_MODEL), qcat, kcat, wi, kmx, B, S)
    return o.reshape(B * S, D_MODEL)


def kernel(x, norm_mix, norm_ffn, ffn_w_gu, ffn_w_down, rw_mix, rw_w_rkv, rw_w0, rw_w1, rw_w2, rw_a0, rw_a1, rw_a2, rw_g1, rw_g2, rw_v0, rw_v1, rw_v2, rw_k_k, rw_k_a, rw_r_k, rw_ln_w, rw_ln_b, rw_w_out, sb_w_qkv, sb_w_out, ds_w_in, ds_q_norm, ds_k_norm, ds_w_out):
    B, S, D = x.shape
    depth = norm_mix.shape[0]
    xf = x.reshape(B * S, D)
    v_first = None
    for i in range(depth):
        kind, j = i % 3, i // 3
        gn = norm_mix[i][None, :]
        if kind == 0:
            v_res = None if j == 0 else (rw_v0[j - 1], rw_v1[j - 1], rw_v2[j - 1])
            o, v_layer = _rwkv_mixer(xf, gn, B, S, rw_mix[j], rw_w_rkv[j], rw_w0[j], rw_w1[j], rw_w2[j],
                                     rw_a0[j], rw_a1[j], rw_a2[j], rw_g1[j], rw_g2[j], rw_k_k[j], rw_k_a[j],
                                     rw_r_k[j], rw_ln_w[j], rw_ln_b[j], v_first, v_res)
            if j == 0:
                v_first = v_layer
            w_out = rw_w_out[j]
        elif kind == 1:
            o = _sb_mixer(xf, gn, B, S, sb_w_qkv[j])
            w_out = sb_w_out[j]
        else:
            o = _dsa_mixer(xf, gn, B, S, ds_w_in[j], ds_q_norm[j], ds_k_norm[j])
            w_out = ds_w_out[j]
        xf = _ffn_layer(xf, o, w_out.astype(BF16), norm_ffn[i][None, :], ffn_w_gu[i].astype(BF16),
                        ffn_w_down[i].astype(BF16))
    return xf.reshape(B, S, D)
```

```python
import functools
import math

import jax
import jax.numpy as jnp
from jax import lax
from jax.experimental import pallas as pl
from jax.experimental.pallas import tpu as pltpu

F32 = jnp.float32
BF16 = jnp.bfloat16
I32 = jnp.int32

D_MODEL = 1024
HEAD_DIM = 64
N_HEADS = D_MODEL // HEAD_DIM
N_PAIRS = N_HEADS // 2
LANES = 128
RMS_EPS = 1e-6
ROPE_THETA = 10000.0
RW_GN_EPS = HEAD_DIM * 1e-5
RW_CHUNK = 64
DS_TOPK_MAX = 256
DS_IDX_HEADS = 8
DS_IDX_DIM = 64
DS_CHUNK = 64
NEG_BIG = -1e30
SB_DEAD_LOG = -105.0
DS_SAFE_LOGIT = 35.0
VMEM_LIMIT = 56 * 1024 * 1024


def _cparams(sem):
    return pltpu.CompilerParams(dimension_semantics=sem, vmem_limit_bytes=VMEM_LIMIT)


def _dot(a, b):
    return jnp.dot(a, b, preferred_element_type=F32)


def _dot_nt(a, b):
    return lax.dot_general(a, b, (((1,), (1,)), ((), ())), preferred_element_type=F32)


def _dot_tn(a, b):
    return lax.dot_general(a, b, (((0,), (0,)), ((), ())), preferred_element_type=F32)


def _split2(x):
    hi = x.astype(BF16)
    lo = (x - hi.astype(F32)).astype(BF16)
    return hi, lo


def _split3(x):
    hi = x.astype(BF16)
    r1 = x - hi.astype(F32)
    mid = r1.astype(BF16)
    lo = (r1 - mid.astype(F32)).astype(BF16)
    return hi, mid, lo


HS_W = 256


def _head_sum(x, hp):
    hi, lo = _split2(x)
    outs = []
    for s in range(x.shape[1] // HS_W):
        sl = slice(s * HS_W, (s + 1) * HS_W)
        outs.append(_dot(jnp.concatenate([hi[:, sl], lo[:, sl]], axis=1), hp))
    return jnp.concatenate(outs, axis=1)


def _rms(x, g):
    ms = jnp.mean(x * x, axis=-1, keepdims=True)
    return x * lax.rsqrt(ms + RMS_EPS) * g


def _ffn_kernel(x_ref, o_ref, wo_ref, g_ref, wgu_ref, wd_ref, out_ref, x1_ref, hn_ref, acc_ref, act_ref, *, tf):
    F = wd_ref.shape[0]
    nf = F // tf
    x1 = x_ref[...] + _dot(o_ref[...], wo_ref[...])
    x1_ref[...] = x1
    hn_ref[...] = _rms(x1, g_ref[...]).astype(BF16)
    for f in range(nf + 1):
        if f < nf:
            hn = hn_ref[...]
            g = _dot(hn, wgu_ref[:, f * tf:(f + 1) * tf])
            u = _dot(hn, wgu_ref[:, F + f * tf:F + (f + 1) * tf])
        if f > 0:
            d = _dot(act_ref[(f - 1) % 2], wd_ref[(f - 1) * tf:f * tf, :])
            if f == 1:
                acc_ref[...] = d
            else:
                acc_ref[...] += d
        if f < nf:
            act_ref[f % 2] = (g * jax.nn.sigmoid(g) * u).astype(BF16)
    out_ref[...] = x1_ref[...] + acc_ref[...]


def _ffn_layer(x, o, w_out, g_ffn, w_gu, w_down, *, tm=512, tf=256):
    T, D = x.shape
    F = w_down.shape[0]
    assert T % tm == 0 and F % tf == 0
    row = lambda i: (i, 0)
    const = lambda i: (0, 0)
    resident = lambda a: pl.BlockSpec(a.shape, const, pipeline_mode=pl.Buffered(1))
    return pl.pallas_call(
        functools.partial(_ffn_kernel, tf=tf),
        grid=(T // tm,),
        in_specs=[
            pl.BlockSpec((tm, D), row),
            pl.BlockSpec((tm, D), row),
            resident(w_out),
            pl.BlockSpec((1, D), const),
            resident(w_gu),
            resident(w_down),
        ],
        out_specs=pl.BlockSpec((tm, D), row),
        out_shape=jax.ShapeDtypeStruct((T, D), F32),
        scratch_shapes=[pltpu.VMEM((tm, D), F32), pltpu.VMEM((tm, D), BF16), pltpu.VMEM((tm, D), F32),
                        pltpu.VMEM((2, tm, tf), BF16)],
        compiler_params=_cparams(("parallel",)),
        name="ffn",
    )(x, o, w_out, g_ffn, w_gu, w_down)


def _norm_proj_kernel(x_ref, g_ref, w_ref, out_ref, hn_ref):
    @pl.when(pl.program_id(1) == 0)
    def _():
        hn_ref[...] = _rms(x_ref[...], g_ref[...]).astype(BF16)

    out_ref[...] = _dot(hn_ref[...], w_ref[...]).astype(out_ref.dtype)


def _norm_proj(x, g, w, *, tm=512, tn=1024):
    T, D = x.shape
    N = w.shape[1]
    return pl.pallas_call(
        _norm_proj_kernel,
        grid=(T // tm, N // tn),
        in_specs=[
            pl.BlockSpec((tm, D), lambda i, j: (i, 0)),
            pl.BlockSpec((1, D), lambda i, j: (0, 0)),
            pl.BlockSpec((D, tn), lambda i, j: (0, j)),
        ],
        out_specs=pl.BlockSpec((tm, tn), lambda i, j: (i, j)),
        out_shape=jax.ShapeDtypeStruct((T, N), BF16),
        scratch_shapes=[pltpu.VMEM((tm, D), BF16)],
        compiler_params=_cparams(("parallel", "arbitrary")),
        name="norm_proj",
    )(x, g, w)


def _sb_kernel(q_ref, k_ref, v_ref, suo_ref, o_ref, c_ref, a_ref, zs_ref, hl_ref, *, tq, tk):
    qi = pl.program_id(2)
    lane = lax.broadcasted_iota(I32, (1, LANES), 1)
    first = lane < HEAD_DIM
    q = q_ref[...] * jnp.asarray(HEAD_DIM ** -0.5, BF16)
    zero = jnp.zeros_like(q)
    qh = (jnp.where(first, q, zero), jnp.where(first, zero, q))
    suo = suo_ref[...]
    c_ref[...] = jnp.zeros_like(c_ref)
    a_ref[...] = jnp.zeros_like(a_ref)
    qpos = qi * tq + lax.broadcasted_iota(I32, (tq, 1), 0)
    def tile(ks, w, masked):
        nsub = w // tk
        ks = pl.multiple_of(ks, tq)
        kb = k_ref[pl.ds(ks, w), :]
        zz = [_dot_nt(qh[h], kb) for h in range(2)]
        if masked:
            earlier = (ks + lax.broadcasted_iota(I32, (1, w), 1)) < qpos
        for h in range(2):
            z = zz[h]
            sp = jnp.maximum(z, 0.0) + jnp.log(1.0 + jnp.exp(-jnp.abs(z)))
            zs = z - sp
            if masked:
                sp = jnp.where(earlier, sp, 0.0)
                zs = jnp.where(earlier, zs, NEG_BIG)
            zs_ref[h, :, :w] = zs
            for u in range(nsub):
                hi, lo = _split2(sp[:, u * tk:(u + 1) * tk])
                hl_ref[h, :, 2 * u * tk:(2 * u + 1) * tk] = hi
                hl_ref[h, :, (2 * u + 1) * tk:(2 * u + 2) * tk] = lo
        cs = [[_dot(hl_ref[h, :, 2 * u * tk:(2 * u + 2) * tk], suo) for u in range(nsub)] for h in range(2)]
        vb = v_ref[pl.ds(ks, w), :]
        zv = jnp.zeros_like(vb)
        vb2 = jnp.concatenate([jnp.where(first, vb, zv), jnp.where(first, zv, vb)], axis=0)
        att = []
        for h in range(2):
            c = c_ref[h]
            parts = [None] * nsub
            for u in reversed(range(nsub)):
                a = jnp.exp(zs_ref[h, :, u * tk:(u + 1) * tk] + cs[h][u][:, :tk] + c)
                parts[u] = a.astype(BF16)
                c = c + cs[h][u][:, tk:]
            c_ref[h] = c
            att += parts
        a_ref[...] += _dot(jnp.concatenate(att, axis=1), vb2)

    def c_max():
        return jnp.max(jnp.maximum(c_ref[0], c_ref[1]))

    s0 = jnp.maximum(qi - 1, 0)
    tile(s0 * tq, 2 * tq, True)

    def keep_going(carry):
        t, cm = carry
        return jnp.logical_and(t < s0, cm > SB_DEAD_LOG)

    def step(carry):
        t, _ = carry
        tile((s0 - 1 - t) * tq, tq, False)
        return t + 1, c_max()

    lax.while_loop(keep_going, step, (jnp.int32(0), c_max()))

    o_ref[...] = a_ref[...].astype(o_ref.dtype)


def _sb_attention(qkv, B, S, *, tq=256, tk=128):
    D = D_MODEL
    r = lax.broadcasted_iota(I32, (tk, 2 * tk), 0)
    c = lax.broadcasted_iota(I32, (tk, 2 * tk), 1)
    suo = jnp.where((c >= tk) | (r > c), -1.0, 0.0).astype(BF16)
    suo = jnp.concatenate([suo, suo], axis=0)
    kern = functools.partial(_sb_kernel, tq=tq, tk=tk)
    return pl.pallas_call(
        kern,
        grid=(B, N_PAIRS, S // tq),
        in_specs=[
            pl.BlockSpec((None, tq, LANES), lambda b, p, i: (b, i, p)),
            pl.BlockSpec((None, S, LANES), lambda b, p, i: (b, 0, N_PAIRS + p)),
            pl.BlockSpec((None, S, LANES), lambda b, p, i: (b, 0, 2 * N_PAIRS + p)),
            pl.BlockSpec((2 * tk, 2 * tk), lambda b, p, i: (0, 0)),
        ],
        out_specs=pl.BlockSpec((None, tq, LANES), lambda b, p, i: (b, i, p)),
        out_shape=jax.ShapeDtypeStruct((B, S, D), BF16),
        scratch_shapes=[pltpu.VMEM((2, tq, tk), F32), pltpu.VMEM((tq, LANES), F32),
                        pltpu.VMEM((2, tq, 2 * tq), F32), pltpu.VMEM((2, tq, 4 * tq), BF16)],
        compiler_params=_cparams(("parallel", "parallel", "arbitrary")),
        name="sb_attn",
    )(qkv, qkv, qkv, suo)


def _rope_slabs(y, cos, sin):
    lane = lax.broadcasted_iota(I32, (1, LANES), 1)
    first = (lane % HEAD_DIM) < (HEAD_DIM // 2)
    outs = []
    for s in range(y.shape[1] // LANES):
        ys = y[:, s * LANES:(s + 1) * LANES]
        rot = jnp.where(first, pltpu.roll(ys, LANES - HEAD_DIM // 2, axis=1), pltpu.roll(ys, HEAD_DIM // 2, axis=1))
        outs.append(ys * cos[:, s * LANES:(s + 1) * LANES] + rot * sin[:, s * LANES:(s + 1) * LANES])
    return jnp.concatenate(outs, axis=1)


def _dsa_proj_kernel(x_ref, g_ref, w_ref, wih_ref, wil_ref, hp_ref, qn_ref, kn_ref, cos_ref, sin_ref,
                     qkv_ref, qcat_ref, kcat_ref, wi_ref, kmx_ref, hh_ref, hl_ref, *, wi_scale):
    j = pl.program_id(1)
    nq = DS_IDX_HEADS * DS_IDX_DIM

    @pl.when(j == 0)
    def _():
        hn = _rms(x_ref[...], g_ref[...])
        hh, hl = _split2(hn)
        hh_ref[...] = hh
        hl_ref[...] = hl
        wih = wih_ref[...]
        idx = _dot(hh, wih) + _dot(hl, wih) + _dot(hh, wil_ref[...])
        cos = cos_ref[...]
        sin = sin_ref[...]
        lane = lax.broadcasted_iota(I32, (1, LANES), 1)
        first = lane < DS_IDX_DIM

        def hi_lo(x):
            hi = x.astype(BF16).astype(F32)
            return hi, x - hi

        qi = _rope_slabs(idx[:, :nq], cos[:, :nq], sin[:, :nq])
        qhi, qlo = hi_lo(qi)
        for h in range(DS_IDX_HEADS):
            sl = slice((h // 2) * LANES, (h // 2 + 1) * LANES)
            if h % 2 == 0:
                own_hi = qhi[:, sl]
                a = jnp.where(first, own_hi, pltpu.roll(qlo[:, sl], DS_IDX_DIM, axis=1))
            else:
                own_hi = pltpu.roll(qhi[:, sl], DS_IDX_DIM, axis=1)
                a = jnp.where(first, own_hi, qlo[:, sl])
            b = jnp.where(first, own_hi, 0.0)
            qcat_ref[:, 2 * h * LANES:(2 * h + 1) * LANES] = a.astype(BF16)
            qcat_ref[:, (2 * h + 1) * LANES:(2 * h + 2) * LANES] = b.astype(BF16)
        tail = idx[:, nq:nq + LANES]
        ki = jnp.where(first, _rope_slabs(tail, cos[:, :LANES], sin[:, :LANES]), 0.0)
        khi, klo = hi_lo(ki)
        kcat_ref[:, :LANES] = (khi + pltpu.roll(khi, DS_IDX_DIM, axis=1)).astype(BF16)
        kcat_ref[:, LANES:] = klo.astype(BF16)
        wi_ref[...] = tail * wi_scale

    y = _dot(hh_ref[...], w_ref[...])

    def normed(gn_ref, scale):
        ms = _head_sum(y * y, hp_ref[...]) * (1.0 / HEAD_DIM)
        yn = y * lax.rsqrt(ms + RMS_EPS) * gn_ref[...]
        out = _rope_slabs(yn, cos_ref[...], sin_ref[...])
        return (out * scale).astype(qkv_ref.dtype)

    @pl.when(j == 0)
    def _():
        qkv_ref[...] = normed(qn_ref, HEAD_DIM ** -0.5)

    @pl.when(j == 1)
    def _():
        kb = normed(kn_ref, 1.0)
        qkv_ref[...] = kb
        kf = kb.astype(F32)
        kmx_ref[...] = jnp.max(_head_sum(kf * kf, hp_ref[...]), axis=0, keepdims=True)

    @pl.when(j == 2)
    def _():
        qkv_ref[...] = y.astype(qkv_ref.dtype)


def _head_sum_matrix():
    r = (lax.broadcasted_iota(I32, (2 * HS_W, HS_W), 0) % HS_W) // HEAD_DIM
    c = lax.broadcasted_iota(I32, (2 * HS_W, HS_W), 1) // HEAD_DIM
    return jnp.where(r == c, 1.0, 0.0).astype(BF16)


def _rope_tables(S):
    half = HEAD_DIM // 2
    inv = 1.0 / (ROPE_THETA ** (jnp.arange(half, dtype=F32) / half))
    ang = jnp.arange(S, dtype=F32)[:, None] * inv[None, :]
    cos = jnp.concatenate([jnp.cos(ang), jnp.cos(ang)], axis=1)
    sin = jnp.concatenate([-jnp.sin(ang), jnp.sin(ang)], axis=1)
    return jnp.tile(cos, (1, N_HEADS)), jnp.tile(sin, (1, N_HEADS))


def _dsa_proj(x, g, w_in, q_norm, k_norm, B, S, *, tm=256):
    T, D = x.shape
    c3 = 3 * D
    nq = DS_IDX_HEADS * DS_IDX_DIM
    n_idx = nq + DS_IDX_DIM + DS_IDX_HEADS
    w_qkv = w_in[:, :c3].astype(BF16)
    w_idx = jnp.pad(w_in[:, c3:], ((0, 0), (0, nq + LANES - n_idx)))
    wih = w_idx.astype(BF16)
    wil = (w_idx - wih.astype(F32)).astype(BF16)
    cos, sin = _rope_tables(S)
    qn = jnp.tile(q_norm, N_HEADS)[None, :]
    kn = jnp.tile(k_norm, N_HEADS)[None, :]
    wi_scale = DS_IDX_HEADS ** -0.5 * DS_IDX_DIM ** -0.5
    nsb = S // tm
    row = lambda i, j: (i, 0)
    const = lambda i, j: (0, 0)
    pos = lambda i, j: (i % nsb, 0)
    return pl.pallas_call(
        functools.partial(_dsa_proj_kernel, wi_scale=wi_scale),
        grid=(T // tm, 3),
        in_specs=[
            pl.BlockSpec((tm, D), row),
            pl.BlockSpec((1, D), const),
            pl.BlockSpec((D, D), lambda i, j: (0, j)),
            pl.BlockSpec((D, nq + LANES), const),
            pl.BlockSpec((D, nq + LANES), const),
            pl.BlockSpec((2 * HS_W, HS_W), const),
            pl.BlockSpec((1, D), const),
            pl.BlockSpec((1, D), const),
            pl.BlockSpec((tm, D), pos),
            pl.BlockSpec((tm, D), pos),
        ],
        out_specs=[
            pl.BlockSpec((tm, D), lambda i, j: (i, j)),
            pl.BlockSpec((tm, 2 * LANES * DS_IDX_HEADS), row),
            pl.BlockSpec((tm, 2 * LANES), row),
            pl.BlockSpec((tm, LANES), row),
            pl.BlockSpec((None, 1, D), lambda i, j: (i, 0, 0)),
        ],
        out_shape=[
            jax.ShapeDtypeStruct((T, c3), BF16),
            jax.ShapeDtypeStruct((T, 2 * LANES * DS_IDX_HEADS), BF16),
            jax.ShapeDtypeStruct((T, 2 * LANES), BF16),
            jax.ShapeDtypeStruct((T, LANES), F32),
            jax.ShapeDtypeStruct((T // tm, 1, D), F32),
        ],
        scratch_shapes=[pltpu.VMEM((tm, D), BF16), pltpu.VMEM((tm, D), BF16)],
        compiler_params=_cparams(("parallel", "arbitrary")),
        name="dsa_proj",
    )(x, g, w_qkv, wih, wil, _head_sum_matrix(), qn, kn, cos, sin)


def _dsa_attn_kernel(q_ref, k_ref, v_ref, qcat_ref, kcat_ref, wi_ref, kmx_ref, tri_ref, o_ref,
                     keys_ref, bias_ref, thr_ref, need_ref, eqb_ref, m_ref, mx_ref, acc_ref,
                     *, tq, ta, rb, n_sel):
    i = pl.program_id(1)
    p = pl.program_id(2)
    lane = lax.broadcasted_iota(I32, (1, LANES), 1)
    row = lax.broadcasted_iota(I32, (tq, 1), 0)
    vis_row = i * tq + (row // DS_CHUNK + 1) * DS_CHUNK
    vis_tile = (i + 1) * tq
    n_a = (vis_tile + ta - 1) // ta

    @pl.when(p == 0)
    def _select():
        wi = wi_ref[...]

        def score_blk(c, _):
            cs = pl.multiple_of(c * ta, ta)
            kc = kcat_ref[pl.ds(cs, ta), :]
            idxs = [_dot_nt(qcat_ref[:, 2 * h * LANES:(2 * h + 2) * LANES], kc) for h in range(DS_IDX_HEADS)]
            sc = jnp.zeros((tq, ta), F32)
            for h in range(DS_IDX_HEADS):
                sc = sc + wi[:, DS_IDX_DIM + h:DS_IDX_DIM + h + 1] * jnp.maximum(idxs[h], 0.0)
            sc = jnp.where(sc == 0.0, 0.0, sc)
            kpos = cs + lax.broadcasted_iota(I32, (1, ta), 1)
            sc = jnp.where(kpos < vis_row, sc, -jnp.inf)
            bits = lax.bitcast_convert_type(sc, I32)
            keys_ref[:, pl.ds(cs, ta)] = jnp.where(bits < 0, bits ^ jnp.int32(0x7FFFFFFF), bits)
            return 0

        int_min = jnp.iinfo(jnp.int32).min
        lax.fori_loop(0, n_a, score_blk, 0)

        @pl.when(n_a % 2 == 1)
        def _():
            keys_ref[:, pl.ds(pl.multiple_of(n_a * ta, ta), ta)] = jnp.full((tq, ta), int_min, I32)

        nsweep = 2 * ta // LANES
        groups = [slice(g * rb, (g + 1) * rb) for g in range(tq // rb)]

        def count(cmps, strict):
            outs = []
            for g0 in range(0, len(groups), 2):
                def blk(c, parts, g0=g0):
                    cs = pl.multiple_of(c * 2 * ta, 2 * ta)
                    parts = list(parts)
                    for s in range(nsweep):
                        for d in range(2):
                            kk = keys_ref[groups[g0 + d], pl.ds(cs + s * LANES, LANES)]
                            hit = (kk > cmps[g0 + d]) if strict else (kk >= cmps[g0 + d])
                            parts[d] = parts[d] + jnp.where(hit, 1, 0).astype(I32)
                    return tuple(parts)
                zero = jnp.zeros((rb, LANES), I32)
                outs += list(lax.fori_loop(0, (n_a + 1) // 2, blk, (zero, zero)))
            return [jnp.sum(pt, axis=1, keepdims=True) for pt in outs]

        thr_ref[...] = jnp.full(thr_ref.shape, int_min, I32)

        def bit_step(t, _):
            bit = lax.shift_left(jnp.int32(1), 31 - t)
            cands = [thr_ref[g, :] + bit for g in groups]
            cnts = count(cands, False)
            for g, cand, cnt in zip(groups, cands, cnts):
                thr_ref[g, :] = jnp.where(cnt >= n_sel, cand, thr_ref[g, :])
            return 0

        lax.fori_loop(0, 32, bit_step, 0)
        nrep = ta // LANES
        n_ge = count([thr_ref[g, :] for g in groups], False)
        tied = sum(jnp.max(jnp.abs(cnt - n_sel)) for cnt in n_ge) > 0

        @pl.when(jnp.logical_not(tied))
        def _():
            def bias_blk(c, _):
                cs = pl.multiple_of(c * ta, ta)
                kk = keys_ref[:, pl.ds(cs, ta)]
                thr = jnp.concatenate([thr_ref[...]] * nrep, axis=1)
                kpos = cs + lax.broadcasted_iota(I32, (1, ta), 1)
                bias_ref[:, pl.ds(cs, ta)] = jnp.where(kpos < vis_row, jnp.where(kk >= thr, 0.0, NEG_BIG), NEG_BIG)
                return 0

            lax.fori_loop(0, n_a, bias_blk, 0)

        @pl.when(tied)
        def _():
            for g, cnt in zip(groups, count([thr_ref[g, :] for g in groups], True)):
                need_ref[g, :] = jnp.broadcast_to((n_sel - cnt).astype(F32), (rb, LANES))
            tri = tri_ref[...]

            def bias_blk(c, _):
                cs = pl.multiple_of(c * ta, ta)
                kk = keys_ref[:, pl.ds(cs, ta)]
                thr = jnp.concatenate([thr_ref[...]] * nrep, axis=1)
                eqf = jnp.where(kk == thr, 1.0, 0.0)
                ranks = _dot(eqf.astype(BF16), tri)
                rank = jnp.concatenate([eqb_ref[...]] * nrep, axis=1) + ranks[:, :ta]
                need = jnp.concatenate([need_ref[...]] * nrep, axis=1)
                sel = jnp.where(kk > thr, 1.0, jnp.where(rank < need, eqf, 0.0))
                kpos = cs + lax.broadcasted_iota(I32, (1, ta), 1)
                sel = jnp.where(kpos < vis_row, sel, 0.0)
                bias_ref[:, pl.ds(cs, ta)] = (sel - 1.0) * (-NEG_BIG)
                eqb_ref[...] += ranks[:, ta:]
                return 0

            eqb_ref[...] = jnp.zeros_like(eqb_ref)
            lax.fori_loop(0, n_a, bias_blk, 0)

        @pl.when(n_a % 2 == 1)
        def _():
            bias_ref[:, pl.ds(pl.multiple_of(n_a * ta, ta), ta)] = jnp.full((tq, ta), NEG_BIG, F32)

    q = q_ref[...]
    zero = jnp.zeros_like(q)
    first = lane < HEAD_DIM
    qh2 = (jnp.where(first, q, zero), jnp.where(first, zero, q))
    n_pair = (n_a + 1) // 2

    trow = lax.broadcasted_iota(I32, (kmx_ref.shape[0], 1), 0)
    kmax2 = jnp.max(jnp.where(trow <= i, kmx_ref[...], 0.0), axis=0, keepdims=True)
    hr = lax.broadcasted_iota(I32, (LANES, LANES), 0) // HEAD_DIM
    hc = lax.broadcasted_iota(I32, (LANES, LANES), 1) // HEAD_DIM
    qf = q.astype(F32)
    q2 = _dot((qf * qf).astype(BF16), jnp.where(hr == hc, 1.0, 0.0).astype(BF16))
    bound2 = 1.05 * jnp.max(q2 * kmax2)
    m_ref[...] = jnp.zeros_like(m_ref)

    @pl.when(bound2 > DS_SAFE_LOGIT * DS_SAFE_LOGIT)
    def _row_max():
        mx_ref[...] = jnp.full(mx_ref.shape, NEG_BIG, F32)

        def max_blk(c2, _):
            for u in range(2):
                cs = pl.multiple_of((2 * c2 + u) * ta, ta)
                kb = k_ref[pl.ds(cs, ta), :]
                bias = bias_ref[:, pl.ds(cs, ta)]
                for h in range(2):
                    mx_ref[u, h] = jnp.maximum(mx_ref[u, h], _dot_nt(qh2[h], kb) + bias)
            return 0

        lax.fori_loop(0, n_pair, max_blk, 0)
        for h in range(2):
            m_ref[h] = jnp.broadcast_to(jnp.max(jnp.maximum(mx_ref[0, h], mx_ref[1, h]), axis=1, keepdims=True),
                                        (tq, LANES))

    ones_h = [jnp.broadcast_to(jnp.where(first, a, b), (ta, LANES)).astype(BF16) for a, b in ((1.0, 0.0), (0.0, 1.0))]
    acc_ref[...] = jnp.zeros_like(acc_ref)

    def attn_blocks(c0, nblk):
        css = [pl.multiple_of((c0 + u) * ta, ta) for u in range(nblk)]
        logits = [[_dot_nt(qh2[h], k_ref[pl.ds(cs, ta), :]) for h in range(2)] for cs in css]
        out = acc_ref[...]
        for u, cs in enumerate(css):
            vb = v_ref[pl.ds(cs, ta), :]
            zv = jnp.zeros_like(vb)
            vs = [jnp.concatenate([jnp.where(first, vb, zv), ones_h[0]], axis=1),
                  jnp.concatenate([jnp.where(first, zv, vb), ones_h[1]], axis=1)]
            bias = bias_ref[:, pl.ds(cs, ta)]
            for h in range(2):
                pr = jnp.exp(logits[u][h] + bias - m_ref[h][:, :1]).astype(BF16)
                out = out + _dot(pr, vs[h])
        acc_ref[...] = out

    n_oct = n_a // 8

    def octet(c8, _):
        attn_blocks(8 * c8, 8)
        return 0

    lax.fori_loop(0, n_oct, octet, 0)
    has_quad = (n_a - 8 * n_oct) >= 4
    base = 8 * n_oct + jnp.where(has_quad, 4, 0)
    rest = n_a - base

    @pl.when(has_quad)
    def _():
        attn_blocks(8 * n_oct, 4)

    @pl.when(rest > 0)
    def _():
        attn_blocks(base, 2)

    @pl.when(rest == 3)
    def _():
        attn_blocks(base + 2, 2)
    acc = acc_ref[...]
    o_ref[...] = (acc[:, :LANES] / acc[:, LANES:]).astype(o_ref.dtype)


def _dsa_attention(qkv, qcat, kcat, wi, kmx, B, S, *, tq=256, ta=512, rb=64):
    D = D_MODEL
    n_sel = min(DS_TOPK_MAX, S // 4)
    assert S % (2 * ta) == 0 and S // LANES <= 256 and kmx.shape[0] * tq == B * S
    nqc = 2 * LANES * DS_IDX_HEADS
    r = lax.broadcasted_iota(I32, (ta, ta + LANES), 0)
    c = lax.broadcasted_iota(I32, (ta, ta + LANES), 1)
    tri = jnp.where((r < c) | (c >= ta), 1.0, 0.0).astype(BF16)
    kern = functools.partial(_dsa_attn_kernel, tq=tq, ta=ta, rb=rb, n_sel=n_sel)
    s_pad = S
    return pl.pallas_call(
        kern,
        grid=(B, S // tq, N_PAIRS),
        in_specs=[
            pl.BlockSpec((None, tq, LANES), lambda b, i, p: (b, i, p)),
            pl.BlockSpec((None, S, LANES), lambda b, i, p: (b, 0, N_PAIRS + p)),
            pl.BlockSpec((None, S, LANES), lambda b, i, p: (b, 0, 2 * N_PAIRS + p)),
            pl.BlockSpec((None, tq, nqc), lambda b, i, p: (b, i, 0)),
            pl.BlockSpec((None, S, 2 * LANES), lambda b, i, p: (b, 0, 0)),
            pl.BlockSpec((None, tq, LANES), lambda b, i, p: (b, i, 0)),
            pl.BlockSpec((None, S // tq, LANES), lambda b, i, p: (b, 0, p)),
            pl.BlockSpec((ta, ta + LANES), lambda b, i, p: (0, 0)),
        ],
        out_specs=pl.BlockSpec((None, tq, LANES), lambda b, i, p: (b, i, p)),
        out_shape=jax.ShapeDtypeStruct((B, S, D), BF16),
        scratch_shapes=[
            pltpu.VMEM((tq, s_pad), I32),
            pltpu.VMEM((tq, s_pad), F32),
            pltpu.VMEM((tq, LANES), I32),
            pltpu.VMEM((tq, LANES), F32),
            pltpu.VMEM((tq, LANES), F32),
            pltpu.VMEM((2, tq, LANES), F32),
            pltpu.VMEM((2, 2, tq, ta), F32),
            pltpu.VMEM((tq, 2 * LANES), F32),
        ],
        compiler_params=_cparams(("parallel", "arbitrary", "arbitrary")),
        name="dsa_attn",
    )(qkv, qkv, qkv, qcat.reshape(B, S, nqc), kcat.reshape(B, S, 2 * LANES), wi.reshape(B, S, LANES),
      kmx.reshape(B, S // tq, D), tri)


def _rw_pre_kernel(*refs, has_vres, seq_tiles):
    if has_vres:
        (x_ref, xp_ref, gn_ref, mix_ref, wr_ref, wk_ref, wv_ref, w0_ref, w1_ref, w2_ref, a0_ref, a1_ref, a2_ref,
         g1_ref, g2_ref, kk_ref, ka_ref, hp_ref, vf_ref, v0_ref, v1_ref, v2_ref,
         r_out, lw_out, k_out, v_out, kn_out, b_out, g_out) = refs
    else:
        (x_ref, xp_ref, gn_ref, mix_ref, wr_ref, wk_ref, wv_ref, w0_ref, w1_ref, w2_ref, a0_ref, a1_ref, a2_ref,
         g1_ref, g2_ref, kk_ref, ka_ref, hp_ref,
         r_out, lw_out, k_out, v_out, kn_out, b_out, g_out) = refs
    i = pl.program_id(0)
    gn = gn_ref[...]
    h = _rms(x_ref[...], gn)
    tm = h.shape[0]
    hprev = _rms(xp_ref[...], gn)[7:8, :]
    hprev = jnp.where(i % seq_tiles == 0, 0.0, hprev)
    rowi = lax.broadcasted_iota(I32, (tm, 1), 0)
    sh = jnp.where(rowi == 0, hprev, pltpu.roll(h, 1, axis=0))
    dlt = sh - h
    mix = mix_ref[...]

    def stream(j):
        return (h + dlt * mix[j:j + 1, :]).astype(BF16)

    r = _dot(stream(0), wr_ref[...])
    k = _dot(stream(1), wk_ref[...])
    xv = stream(2)
    v = _dot(xv, wv_ref[...])
    w = w0_ref[...] + _dot(jnp.tanh(_dot(stream(3), w1_ref[...])).astype(BF16), w2_ref[...])
    a = jax.nn.sigmoid(a0_ref[...] + _dot(_dot(stream(4), a1_ref[...]).astype(BF16), a2_ref[...]))
    g = _dot(jax.nn.sigmoid(_dot(stream(5), g1_ref[...])).astype(BF16), g2_ref[...])
    if has_vres:
        gate = jax.nn.sigmoid(v0_ref[...] + _dot(_dot(xv, v1_ref[...]).astype(BF16), v2_ref[...]))
        v = v + (vf_ref[...] - v) * gate
    logw = -(jnp.maximum(-w, 0.0) + jnp.log1p(jnp.exp(-jnp.abs(w)))) - 0.5
    lw_out[...] = -jnp.exp(logw)
    kk = k * kk_ref[...]
    ss = _head_sum(kk * kk, hp_ref[...])
    kk = kk * lax.rsqrt(jnp.maximum(ss, 1e-24))
    r_out[...] = r
    k_out[...] = k * (1.0 + (a - 1.0) * ka_ref[...])
    v_out[...] = v
    kn_out[...] = kk
    b_out[...] = kk * a
    g_out[...] = g


def _pad_cols(w, n):
    return jnp.pad(w, ((0, 0), (0, n - w.shape[1])))


def _pad_rows(w, n):
    return jnp.pad(w, ((0, n - w.shape[0]), (0, 0)))


def _rw_pre(x, gn, mix, w_rkv, w0, w1, w2, a0, a1, a2, g1, g2, k_k, k_a, v_first, v_res, S, *, tm=256):
    T, D = x.shape
    has_vres = v_res is not None
    row = lambda i: (i, 0)
    const = lambda i: (0, 0)

    def lora(wa, wb):
        n = -(-wa.shape[1] // LANES) * LANES
        return _pad_cols(wa, n).astype(BF16), _pad_rows(wb, n).astype(BF16)

    w1p, w2p = lora(w1, w2)
    a1p, a2p = lora(a1, a2)
    g1p, g2p = lora(g1, g2)
    args = [x, x, gn, mix, w_rkv[0].astype(BF16), w_rkv[1].astype(BF16), w_rkv[2].astype(BF16),
            w0[None, :], w1p, w2p, a0[None, :], a1p, a2p, g1p, g2p, k_k[None, :], k_a[None, :], _head_sum_matrix()]
    full = lambda a: pl.BlockSpec(a.shape, const)
    in_specs = [pl.BlockSpec((tm, D), row),
                pl.BlockSpec((8, D), lambda i: (jnp.maximum(i * (tm // 8) - 1, 0), 0))]
    in_specs += [full(a) for a in args[2:]]
    if has_vres:
        v0, v1, v2 = v_res
        v1p, v2p = lora(v1, v2)
        extra = [v_first, v0[None, :], v1p, v2p]
        args += extra
        in_specs += [pl.BlockSpec((tm, D), row)] + [full(a) for a in extra[1:]]
    outs = pl.pallas_call(
        functools.partial(_rw_pre_kernel, has_vres=has_vres, seq_tiles=S // tm),
        grid=(T // tm,),
        in_specs=in_specs,
        out_specs=[pl.BlockSpec((tm, D), row)] * 7,
        out_shape=[jax.ShapeDtypeStruct((T, D), F32)] * 7,
        compiler_params=_cparams(("parallel",)),
        name="rw_pre",
    )(*args)
    return outs


def _rw_scan_kernel(r_ref, lw_ref, k_ref, v_ref, kn_ref, b_ref, y_ref, s_ref):
    L = RW_CHUNK

    @pl.when(pl.program_id(1) == 0)
    def _():
        s_ref[...] = jnp.zeros_like(s_ref)

    nb = r_ref.shape[0]

    def rows(ref):
        return jnp.concatenate([ref[s] for s in range(nb)], axis=0)

    lw = rows(lw_ref)
    tr = lax.broadcasted_iota(I32, (nb * L, nb * L), 0)
    tc = lax.broadcasted_iota(I32, (nb * L, nb * L), 1)
    tri = jnp.where((tr >= tc) & (tr // L == tc // L), 1.0, 0.0).astype(BF16)
    hi, mid, lo = _split3(lw)
    lwc = _dot(tri, hi) + _dot(tri, mid) + _dot(tri, lo)
    wl = jnp.concatenate([jnp.broadcast_to(lwc[(s + 1) * L - 1:(s + 1) * L, :], (L, lwc.shape[1]))
                          for s in range(nb)], axis=0)
    w_inc = jnp.exp(lwc)
    w_exc = jnp.exp(lwc - lw)
    w_inv = jnp.exp(-lwc)
    w_end = jnp.exp(wl - lwc)
    w_all = jnp.exp(wl)
    kv = rows(k_ref)
    bv = rows(b_ref)
    vv = rows(v_ref)
    rt = rows(r_ref) * w_inc
    at = -rows(kn_ref) * w_exc
    kt = kv * w_inv
    bt = bv * w_inv
    ke = kv * w_end
    be = bv * w_end

    lane = lax.broadcasted_iota(I32, (1, LANES), 1)
    first = lane < HEAD_DIM

    def sm(x):
        return jnp.concatenate([jnp.where(first, x, 0.0), jnp.where(first, 0.0, x)], axis=0)

    n2 = 2 * L
    ri = lax.broadcasted_iota(I32, (2 * n2, 2 * n2), 0)
    ci = lax.broadcasted_iota(I32, (2 * n2, 2 * n2), 1)
    causal = (ri % L) + ri // n2 > (ci % L)
    bi = lax.broadcasted_iota(I32, (n2, n2), 0) // L
    bj = lax.broadcasted_iota(I32, (n2, n2), 1) // L
    blockdiag = bi == bj
    eye = jnp.where(lax.broadcasted_iota(I32, (n2, n2), 0) == lax.broadcasted_iota(I32, (n2, n2), 1), 1.0, 0.0)

    ents = [(s, p) for s in range(nb) for p in range(N_PAIRS)]
    prs = range(len(ents))
    cut = [(slice(s * L, (s + 1) * L), slice(p * LANES, (p + 1) * LANES)) for s, p in ents]
    vsb = [sm(vv[c]).astype(BF16) for c in cut]
    a_l = [jnp.concatenate([sm(at[c]), sm(rt[c])], axis=0).astype(BF16) for c in cut]
    a_r = [jnp.concatenate([sm(bt[c]), sm(kt[c])], axis=0).astype(BF16) for c in cut]
    m = [jnp.where(causal, _dot_nt(a_l[e], a_r[e]), 0.0) for e in prs]
    s_old = [s_ref[e] for e in prs]
    p12 = [_dot_nt(a_l[e], s_old[e].astype(BF16)) for e in prs]
    pm = [m[e][:n2, :n2].astype(BF16) for e in prs]
    tm = [eye + m[e][:n2, :n2] for e in prs]
    for _ in range(5):
        pm = [_dot(pm[e], pm[e]).astype(BF16) for e in prs]
        tm = [tm[e] + _dot(tm[e].astype(BF16), pm[e]) for e in prs]
    z = [p12[e][:n2] + _dot(m[e][:n2, n2:].astype(BF16), vsb[e]) for e in prs]
    u = [_dot(tm[e].astype(BF16), z[e].astype(BF16)) for e in prs]
    y = [p12[e][n2:] + _dot(m[e][n2:, :].astype(BF16), jnp.concatenate([u[e].astype(BF16), vsb[e]], axis=0))
         for e in prs]
    upd = [_dot_tn(jnp.concatenate([u[e][:L] + u[e][L:], vv[cut[e]]], axis=0).astype(BF16),
                   jnp.concatenate([be[cut[e]], ke[cut[e]]], axis=0).astype(BF16)) for e in prs]
    for e, (s, p) in enumerate(ents):
        y_ref[s, :, cut[e][1]] = y[e][:L] + y[e][L:]
        s_ref[e] = s_old[e] * w_all[s * L:s * L + 1, cut[e][1]] + jnp.where(blockdiag, upd[e], 0.0)


def _rw_scan(r, lw, k, v, kn, b, B, S):
    D = D_MODEL
    L = RW_CHUNK
    nb = 2 if B % 2 == 0 else 1
    spec = pl.BlockSpec((nb, L, D), lambda bb, c: (bb, c, 0))
    args = [a.reshape(B, S, D) for a in (r, lw, k, v, kn, b)]
    y = pl.pallas_call(
        _rw_scan_kernel,
        grid=(B // nb, S // L),
        in_specs=[spec] * 6,
        out_specs=spec,
        out_shape=jax.ShapeDtypeStruct((B, S, D), F32),
        scratch_shapes=[pltpu.VMEM((nb * N_PAIRS, LANES, LANES), F32)],
        compiler_params=_cparams(("parallel", "arbitrary")),
        name="rw_scan",
    )(*args)
    return y.reshape(B * S, D)


def _rw_post_kernel(y_ref, r_ref, k_ref, v_ref, g_ref, rk_ref, lnw_ref, lnb_ref, hp_ref, o_ref):
    hp = hp_ref[...]
    y = y_ref[...]
    inv = 1.0 / HEAD_DIM
    mu = _head_sum(y, hp) * inv
    yc = y - mu
    var = _head_sum(yc * yc, hp) * inv
    yn = yc * lax.rsqrt(var + RW_GN_EPS) * lnw_ref[...] + lnb_ref[...]
    bonus = _head_sum(r_ref[...] * k_ref[...] * rk_ref[...], hp) * v_ref[...]
    o_ref[...] = ((yn + bonus) * g_ref[...]).astype(o_ref.dtype)


def _rw_post(y, r, k, v, g, r_k, ln_w, ln_b, *, tm=256):
    T, D = y.shape
    row = lambda i: (i, 0)
    const = lambda i: (0, 0)
    return pl.pallas_call(
        _rw_post_kernel,
        grid=(T // tm,),
        in_specs=[pl.BlockSpec((tm, D), row)] * 5 + [pl.BlockSpec((1, D), const)] * 3
        + [pl.BlockSpec((2 * HS_W, HS_W), const)],
        out_specs=pl.BlockSpec((tm, D), row),
        out_shape=jax.ShapeDtypeStruct((T, D), BF16),
        compiler_params=_cparams(("parallel",)),
        name="rw_post",
    )(y, r, k, v, g, r_k.reshape(1, D), ln_w[None, :], ln_b[None, :], _head_sum_matrix())


def _rwkv_mixer(x, gn, B, S, mix, w_rkv, w0, w1, w2, a0, a1, a2, g1, g2, k_k, k_a, r_k, ln_w, ln_b, v_first, v_res):
    r, lw, k, v, kn, b, g = _rw_pre(x, gn, mix, w_rkv, w0, w1, w2, a0, a1, a2, g1, g2, k_k, k_a, v_first, v_res, S)
    y = _rw_scan(r, lw, k, v, kn, b, B, S)
    return _rw_post(y, r, k, v, g, r_k, ln_w, ln_b), v


def _sb_mixer(x, gn, B, S, w_qkv):
    qkv = _norm_proj(x, gn, w_qkv.astype(BF16))
    return _sb_attention(qkv.reshape(B, S, 3 * D_MODEL), B, S).reshape(B * S, D_MODEL)


def _dsa_mixer(x, gn, B, S, w_in, q_norm, k_norm):
    qkv, qcat, kcat, wi, kmx = _dsa_proj(x, gn, w_in, q_norm, k_norm, B, S)
    o = _dsa_attention(qkv.reshape(B, S, 3 * D_MODEL), qcat, kcat, wi, kmx, B, S)
    return o.reshape(B * S, D_MODEL)


def kernel(x, norm_mix, norm_ffn, ffn_w_gu, ffn_w_down, rw_mix, rw_w_rkv, rw_w0, rw_w1, rw_w2, rw_a0, rw_a1, rw_a2, rw_g1, rw_g2, rw_v0, rw_v1, rw_v2, rw_k_k, rw_k_a, rw_r_k, rw_ln_w, rw_ln_b, rw_w_out, sb_w_qkv, sb_w_out, ds_w_in, ds_q_norm, ds_k_norm, ds_w_out):
    B, S, D = x.shape
    depth = norm_mix.shape[0]
    xf = x.reshape(B * S, D)
    v_first = None
    for i in range(depth):
        kind, j = i % 3, i // 3
        gn = norm_mix[i][None, :]
        if kind == 0:
            v_res = None if j == 0 else (rw_v0[j - 1], rw_v1[j - 1], rw_v2[j - 1])
            o, v_layer = _rwkv_mixer(xf, gn, B, S, rw_mix[j], rw_w_rkv[j], rw_w0[j], rw_w1[j], rw_w2[j],
                                     rw_a0[j], rw_a1[j], rw_a2[j], rw_g1[j], rw_g2[j], rw_k_k[j], rw_k_a[j],
                                     rw_r_k[j], rw_ln_w[j], rw_ln_b[j], v_first, v_res)
            if j == 0:
                v_first = v_layer
            w_out = rw_w_out[j]
        elif kind == 1:
            o = _sb_mixer(xf, gn, B, S, sb_w_qkv[j])
            w_out = sb_w_out[j]
        else:
            o = _dsa_mixer(xf, gn, B, S, ds_w_in[j], ds_q_norm[j], ds_k_norm[j])
            w_out = ds_w_out[j]
        xf = _ffn_layer(xf, o, w_out.astype(BF16), norm_ffn[i][None, :], ffn_w_gu[i].astype(BF16),
                        ffn_w_down[i].astype(BF16))
    return xf.reshape(B, S, D)
```
